```python
import math
import jax, jax.numpy as jnp
from jax import lax
import numpy as np

D_MODEL = 1024
BATCH = 8
SEQ = 2048
DEPTH = 2
DEC_BATCH = 8
DEC_SEQ = 16
PAST_LEN = 2048

CHUNK = 64
N_EVEN = (DEPTH + 1) // 2
N_ODD = DEPTH // 2
MLP_CHUNK = 128
A_GROUPS = 4
A_WIDTH = D_MODEL // 2
A_GC = A_WIDTH // A_GROUPS
POOL_WINDOWS = (2, 4, 8, 16)
B_WIDTH = D_MODEL // 2
B_GC = B_WIDTH // len(POOL_WINDOWS)
POOL_PAD = max(POOL_WINDOWS) - 1
IN_AB = 2 * A_WIDTH + B_WIDTH
OUT_AB = A_WIDTH + B_WIDTH
SB_HEADS = 16
SB_HEAD_DIM = D_MODEL // SB_HEADS
Q_BLOCK = 128
D_FF = -(-8 * D_MODEL // (3 * 256)) * 256
EPS = 1e-6

kernel_name = "stream_gmlp_pool_stickbreak_step"


def _chunk_mask(L):
    i = np.arange(L)
    return (i[None, :] // CHUNK) <= (i[:, None] // CHUNK)


def rms_norm(x, g):
    xf = x.astype(jnp.float32)
    y = xf * lax.rsqrt(jnp.mean(xf * xf, axis=-1, keepdims=True) + EPS)
    return (y * g.astype(jnp.float32)).astype(x.dtype)


def layer_norm(x, g, b):
    xf = x.astype(jnp.float32)
    mu = jnp.mean(xf, axis=-1, keepdims=True)
    xc = xf - mu
    y = xc * lax.rsqrt(jnp.mean(xc * xc, axis=-1, keepdims=True) + EPS)
    return (y * g.astype(jnp.float32) + b.astype(jnp.float32)).astype(x.dtype)


def swiglu(x, wg, wu, wd):
    return (jax.nn.silu(x @ wg) * (x @ wu)) @ wd


def spatial_gating(u, v_n, w_s, b_s, L):
    B, T, _ = v_n.shape
    n = T // L
    w = jnp.where(_chunk_mask(L)[None], w_s[:, :L, :L], 0.0)
    vv = v_n.reshape(B, n, L, A_GROUPS, A_GC)
    s = jnp.einsum('gij,bnjgc->bnigc', w, vv) + b_s[:, :L].T[None, None, :, :, None]
    return u * s.reshape(B, T, A_WIDTH)


def pool_mixer(p, prev, start_pos, w_map, scale):
    B, T, C = p.shape
    full = jnp.concatenate([prev, p], axis=1)
    ff = full.astype(jnp.float32)
    cs = jnp.concatenate([jnp.zeros((B, 1, C), jnp.float32), jnp.cumsum(ff, axis=1)], axis=1)
    end = cs[:, POOL_PAD + 1:]
    pos = start_pos + jnp.arange(T)
    pf = p.astype(jnp.float32)
    outs = []
    for g, w in enumerate(POOL_WINDOWS):
        sl = slice(g * B_GC, (g + 1) * B_GC)
        win = end[..., sl] - cs[:, POOL_PAD + 1 - w:POOL_PAD + 1 - w + T, sl]
        cnt = jnp.minimum(w, pos + 1).astype(jnp.float32)
        outs.append(win / cnt[None, :, None] - pf[..., sl])
    d = jnp.stack(outs, axis=2)
    y = jnp.einsum('btgc,gce->btge', d, w_map.astype(jnp.float32)).reshape(B, T, C)
    y = (y * scale.astype(jnp.float32)).astype(p.dtype)
    return y, full[:, -POOL_PAD:]


def mixer_ab(h, prev_pool, start_pos, L, w_in, ln_g, ln_b, w_s, b_s, w_map, p_scale, w_out):
    z = h @ w_in
    uv = jax.nn.gelu(z[..., :2 * A_WIDTH], approximate=False)
    u, v = uv[..., :A_WIDTH], uv[..., A_WIDTH:]
    v_n = layer_norm(v, ln_g, ln_b)
    a_out = spatial_gating(u, v_n, w_s, b_s, L)
    b_out, tail = pool_mixer(z[..., 2 * A_WIDTH:], prev_pool, start_pos, w_map, p_scale)
    y = jnp.concatenate([a_out, b_out], axis=-1) @ w_out
    return y, tail, v_n


def stick_breaking(q, k, v, q_pos, k_pos):
    z = jnp.einsum('bqhd,bkhd->bhqk', q.astype(jnp.float32), k.astype(jnp.float32)) * (SB_HEAD_DIM ** -0.5)
    causal = (k_pos[None, :] < q_pos[:, None])[None, None]
    log_1m = jnp.where(causal, jax.nn.log_sigmoid(-z), 0.0)
    after = lax.cumsum(log_1m, axis=3, reverse=True) - log_1m
    a = jnp.where(causal, jnp.exp(jax.nn.log_sigmoid(z) + after), 0.0)
    return jnp.einsum('bhqk,bkhd->bqhd', a, v.astype(jnp.float32)).astype(q.dtype)


def stick_breaking_prompt(q, k, v):
    B, S, H, Dh = q.shape
    nb = S // Q_BLOCK
    qb = q.reshape(B, nb, Q_BLOCK, H, Dh).transpose(1, 0, 2, 3, 4)
    qpos = jnp.arange(S).reshape(nb, Q_BLOCK)
    kpos = jnp.arange(S)
    ob = lax.map(lambda xs: stick_breaking(xs[0], k, v, xs[1], kpos), (qb, qpos))
    return ob.transpose(1, 0, 2, 3, 4).reshape(B, S, H, Dh)


def mixer_c(h, k_prev, v_prev, w_qkv, w_o, is_prompt):
    B, T, _ = h.shape
    qkv = (h @ w_qkv).reshape(B, T, 3, SB_HEADS, SB_HEAD_DIM)
    q, k, v = qkv[:, :, 0], qkv[:, :, 1], qkv[:, :, 2]
    if is_prompt:
        o = stick_breaking_prompt(q, k, v)
    else:
        P = k_prev.shape[1]
        k_all = jnp.concatenate([k_prev, k], axis=1)
        v_all = jnp.concatenate([v_prev, v], axis=1)
        o = stick_breaking(q, k_all, v_all, P + jnp.arange(T), jnp.arange(P + T))
    return o.reshape(B, T, D_MODEL) @ w_o, k, v


def trunk(x, is_prompt, state_pool, cache_k, cache_v, p):
    B, T, _ = x.shape
    tails, ks, vs, vns = [], [], [], []
    for layer in range(DEPTH):
        i = layer // 2
        h = rms_norm(x, p['norm_mix_pre'][layer])
        if layer % 2 == 0:
            if is_prompt:
                prev, start, L = jnp.zeros((B, POOL_PAD, B_WIDTH), x.dtype), 0, MLP_CHUNK
            else:
                prev, start, L = state_pool[i], PAST_LEN, T
            y, tail, v_n = mixer_ab(h, prev, start, L, p['w_in_ab'][i], p['ln_v_g'][i], p['ln_v_b'][i],
                                    p['w_spatial'][i], p['b_spatial'][i], p['w_pool_map'][i],
                                    p['pool_scale'][i], p['w_out_ab'][i])
            tails.append(tail)
            vns.append(v_n)
        else:
            kp = None if is_prompt else cache_k[i]
            vp = None if is_prompt else cache_v[i]
            y, k, v = mixer_c(h, kp, vp, p['w_qkv'][i], p['w_o_sb'][i], is_prompt)
            ks.append(k)
            vs.append(v)
        x = x + rms_norm(y, p['norm_mix_post'][layer])
        h = rms_norm(x, p['norm_ffn_pre'][layer])
        f = swiglu(h, p['w_gate'][layer], p['w_up'][layer], p['w_down'][layer])
        x = x + rms_norm(f, p['norm_ffn_post'][layer])
    return x, jnp.stack(tails), jnp.stack(ks), jnp.stack(vs), jnp.stack(vns)


def setup_inputs(seed: int = 0) -> dict:
    key = jax.random.key(seed)
    k = jax.random.split(key, 24)
    f32 = jnp.float32
    nrm = lambda kk, shape, s: jax.random.normal(kk, shape, f32) * s
    gain = lambda kk, shape: 1.0 + nrm(kk, shape, 0.05)
    return {
        'x_prompt': nrm(k[0], (BATCH, SEQ, D_MODEL), 1.0),
        'x_sample': nrm(k[1], (DEC_BATCH, DEC_SEQ, D_MODEL), 1.0),
        'state_pool': nrm(k[2], (N_EVEN, DEC_BATCH, POOL_PAD, B_WIDTH), 1.0),
        'cache_k': nrm(k[3], (N_ODD, DEC_BATCH, PAST_LEN, SB_HEADS, SB_HEAD_DIM), 1.0),
        'cache_v': nrm(k[4], (N_ODD, DEC_BATCH, PAST_LEN, SB_HEADS, SB_HEAD_DIM), 1.0),
        'norm_mix_pre': gain(k[5], (DEPTH, D_MODEL)),
        'norm_mix_post': gain(k[6], (DEPTH, D_MODEL)),
        'norm_ffn_pre': gain(k[7], (DEPTH, D_MODEL)),
        'norm_ffn_post': gain(k[8], (DEPTH, D_MODEL)),
        'w_in_ab': nrm(k[9], (N_EVEN, D_MODEL, IN_AB), D_MODEL ** -0.5),
        'ln_v_g': gain(k[10], (N_EVEN, A_WIDTH)),
        'ln_v_b': nrm(k[11], (N_EVEN, A_WIDTH), 0.02),
        'w_spatial': nrm(k[12], (N_EVEN, A_GROUPS, MLP_CHUNK, MLP_CHUNK), MLP_CHUNK ** -0.5),
        'b_spatial': nrm(k[13], (N_EVEN, A_GROUPS, MLP_CHUNK), 0.02),
        'w_pool_map': nrm(k[14], (N_EVEN, len(POOL_WINDOWS), B_GC, B_GC), B_GC ** -0.5),
        'pool_scale': 1.0 + nrm(k[15], (N_EVEN, B_WIDTH), 0.1),
        'w_out_ab': nrm(k[16], (N_EVEN, OUT_AB, D_MODEL), OUT_AB ** -0.5),
        'w_qkv': nrm(k[17], (N_ODD, D_MODEL, 3 * D_MODEL), D_MODEL ** -0.5),
        'w_o_sb': nrm(k[18], (N_ODD, D_MODEL, D_MODEL), D_MODEL ** -0.5),
        'w_gate': nrm(k[19], (DEPTH, D_MODEL, D_FF), D_MODEL ** -0.5),
        'w_up': nrm(k[20], (DEPTH, D_MODEL, D_FF), D_MODEL ** -0.5),
        'w_down': nrm(k[21], (DEPTH, D_FF, D_MODEL), D_FF ** -0.5),
    }


def reference(x_prompt, x_sample, state_pool, cache_k, cache_v, norm_mix_pre, norm_mix_post,
              norm_ffn_pre, norm_ffn_post, w_in_ab, ln_v_g, ln_v_b, w_spatial, b_spatial,
              w_pool_map, pool_scale, w_out_ab, w_qkv, w_o_sb, w_gate, w_up, w_down):
    p = dict(norm_mix_pre=norm_mix_pre, norm_mix_post=norm_mix_post, norm_ffn_pre=norm_ffn_pre,
             norm_ffn_post=norm_ffn_post, w_in_ab=w_in_ab, ln_v_g=ln_v_g, ln_v_b=ln_v_b,
             w_spatial=w_spatial, b_spatial=b_spatial, w_pool_map=w_pool_map, pool_scale=pool_scale,
             w_out_ab=w_out_ab, w_qkv=w_qkv, w_o_sb=w_o_sb, w_gate=w_gate, w_up=w_up, w_down=w_down)
    y_prompt, pool_p, k_p, v_p, _ = trunk(x_prompt, True, state_pool, cache_k, cache_v, p)
    y_sample, pool_s, k_s, v_s, vn_s = trunk(x_sample, False, state_pool, cache_k, cache_v, p)
    return (y_prompt, y_sample, pool_p, k_p, v_p, pool_s, k_s, v_s, vn_s)
```

```python
import functools

import jax
import jax.numpy as jnp
from jax import lax
from jax.experimental import pallas as pl
from jax.experimental.pallas import tpu as pltpu

F32 = jnp.float32
BF16 = jnp.bfloat16

EPS = 1e-6
CHUNK = 64
POOL_WINDOWS = (2, 4, 8, 16)
POOL_HALO = 16
LANES = 128
HEAD_DIM = 64
ATTN_BLOCK = 256
VMEM_LIMIT = 56 * 1024 * 1024


def _rms(x, g):
    return x * lax.rsqrt(jnp.mean(x * x, axis=-1, keepdims=True) + EPS) * g


def _dot(a, b):
    return jnp.dot(a, b, preferred_element_type=F32)


def _params(*sem):
    return pltpu.CompilerParams(dimension_semantics=sem, vmem_limit_bytes=VMEM_LIMIT)


def _full(shape):
    return pl.BlockSpec(shape, lambda *_: (0,) * len(shape))


def _mixer_ab_kernel(x_ref, prev_ref, gpre_ref, gpost_ref, win_ref, lng_ref, lnb_ref, ws_ref,
                     bs_ref, wmap_ref, pscale_ref, wout_ref, *rest,
                     tm, chunk, tiles_per_batch, start_pos, emit_vn):
    if emit_vn:
        x1_ref, tail_ref, vn_ref, ext_ref, cat_ref = rest
    else:
        x1_ref, tail_ref, ext_ref, cat_ref = rest
    aw = 4 * LANES
    tib = pl.program_id(0) % tiles_per_batch

    @pl.when(tib == 0)
    def _():
        ext_ref[0:POOL_HALO, :] = prev_ref[...]

    x = x_ref[...]
    h = _rms(x, gpre_ref[...]).astype(BF16)
    z = _dot(h, win_ref[...])
    zuv = z[:, :2 * aw]
    uv = 0.5 * zuv * (1.0 + lax.erf(zuv * (0.5 ** 0.5)))
    u, v = uv[:, :aw], uv[:, aw:]
    mu = jnp.mean(v, axis=-1, keepdims=True)
    vc = v - mu
    vn = vc * lax.rsqrt(jnp.mean(vc * vc, axis=-1, keepdims=True) + EPS) * lng_ref[...] + lnb_ref[...]
    if emit_vn:
        vn_ref[...] = vn

    ri = lax.broadcasted_iota(jnp.int32, (chunk, LANES), 0)
    ci = lax.broadcasted_iota(jnp.int32, (chunk, LANES), 1)
    allowed = (ci // CHUNK <= ri // CHUNK) & (ci < chunk)
    bias = bs_ref[...]
    vnb = vn.astype(BF16)
    for g in range(4):
        wg = jnp.where(allowed, ws_ref[g, 0:chunk, :], 0.0).astype(BF16)
        bg = bias[:, g:g + 1]
        for c in range(tm // chunk):
            vblk = vnb[c * chunk:(c + 1) * chunk, g * LANES:(g + 1) * LANES]
            if chunk < LANES:
                vblk = jnp.concatenate([vblk, jnp.zeros((LANES - chunk, LANES), BF16)], axis=0)
            s = _dot(wg, vblk) + bg
            ublk = u[c * chunk:(c + 1) * chunk, g * LANES:(g + 1) * LANES]
            cat_ref[c * chunk:(c + 1) * chunk, g * LANES:(g + 1) * LANES] = (ublk * s).astype(BF16)

    p = z[:, 2 * aw:]
    ext_ref[POOL_HALO:POOL_HALO + tm, :] = p
    e1 = ext_ref[...]
    e2 = e1 + pltpu.roll(e1, 1, axis=0)
    e4 = e2[:, LANES:] + pltpu.roll(e2[:, LANES:], 2, axis=0)
    e8 = e4[:, LANES:] + pltpu.roll(e4[:, LANES:], 4, axis=0)
    e16 = e8[:, LANES:] + pltpu.roll(e8[:, LANES:], 8, axis=0)
    wins = (e2[POOL_HALO:, :LANES], e4[POOL_HALO:, :LANES], e8[POOL_HALO:, :LANES], e16[POOL_HALO:, :])
    pos = start_pos + tib * tm + lax.broadcasted_iota(jnp.int32, (tm, 1), 0)
    pscale = pscale_ref[...]
    for g, w in enumerate(POOL_WINDOWS):
        inv_cnt = 1.0 / jnp.minimum(w, pos + 1).astype(F32)
        d = wins[g] * inv_cnt - p[:, g * LANES:(g + 1) * LANES]
        yb = _dot(d.astype(BF16), wmap_ref[g]) * pscale[:, g * LANES:(g + 1) * LANES]
        cat_ref[:, aw + g * LANES:aw + (g + 1) * LANES] = yb.astype(BF16)

    tail = ext_ref[tm:tm + POOL_HALO, :]
    ext_ref[0:POOL_HALO, :] = tail

    @pl.when(tib == tiles_per_batch - 1)
    def _():
        tail_ref[...] = tail

    y = _dot(cat_ref[...], wout_ref[...])
    x1_ref[...] = x + _rms(y, gpost_ref[...])


def _mixer_ab(x, prev, gpre, gpost, w_in, ln_g, ln_b, w_s, b_s, w_map, p_scale, w_out,
              *, batch, tm, chunk, start_pos, emit_vn):
    t, d = x.shape
    aw = 4 * LANES
    tiles_per_batch = t // batch // tm
    kern = functools.partial(_mixer_ab_kernel, tm=tm, chunk=chunk, tiles_per_batch=tiles_per_batch,
                             start_pos=start_pos, emit_vn=emit_vn)
    out_shape = [jax.ShapeDtypeStruct((t, d), F32), jax.ShapeDtypeStruct((batch, POOL_HALO, aw), F32)]
    out_specs = [pl.BlockSpec((tm, d), lambda i: (i, 0)),
                 pl.BlockSpec((None, POOL_HALO, aw), lambda i: (i // tiles_per_batch, 0, 0))]
    if emit_vn:
        out_shape.append(jax.ShapeDtypeStruct((t, aw), F32))
        out_specs.append(pl.BlockSpec((tm, aw), lambda i: (i, 0)))
    return pl.pallas_call(
        kern,
        out_shape=out_shape,
        grid=(t // tm,),
        in_specs=[
            pl.BlockSpec((tm, d), lambda i: (i, 0)),
            pl.BlockSpec((None, POOL_HALO, aw), lambda i: (i // tiles_per_batch, 0, 0)),
            _full((1, d)), _full((1, d)), _full(w_in.shape), _full((1, aw)), _full((1, aw)),
            _full(w_s.shape), _full(b_s.shape), _full(w_map.shape), _full((1, aw)), _full(w_out.shape),
        ],
        out_specs=out_specs,
        scratch_shapes=[pltpu.VMEM((tm + POOL_HALO, aw), F32), pltpu.VMEM((tm, 2 * aw), BF16)],
        compiler_params=_params("arbitrary"),
        name="mixer_ab",
    )(x, prev, gpre, gpost, w_in, ln_g, ln_b, w_s, b_s, w_map, p_scale, w_out)


def _ffn_kernel(*refs, ff_chunk, has_proj):
    if has_proj:
        x_ref, o_ref, wo_ref, gmix_ref, gpre_ref, gpost_ref, wg_ref, wu_ref, wd_ref, out_ref = refs
        x = x_ref[...] + _rms(_dot(o_ref[...], wo_ref[...]), gmix_ref[...])
    else:
        x_ref, gpre_ref, gpost_ref, wg_ref, wu_ref, wd_ref, out_ref = refs
        x = x_ref[...]
    h = _rms(x, gpre_ref[...]).astype(BF16)
    d_ff = wg_ref.shape[1]
    f = None
    for c0 in range(0, d_ff, ff_chunk):
        c1 = min(c0 + ff_chunk, d_ff)
        act = jax.nn.silu(_dot(h, wg_ref[:, c0:c1])) * _dot(h, wu_ref[:, c0:c1])
        part = _dot(act.astype(BF16), wd_ref[c0:c1, :])
        f = part if f is None else f + part
    out_ref[...] = x + _rms(f, gpost_ref[...])


def _ffn(x, gpre, gpost, wg, wu, wd, *, tm, proj=None):
    t, d = x.shape
    row = pl.BlockSpec((tm, d), lambda i: (i, 0))
    args, specs = [x], [row]
    if proj is not None:
        o, wo, gmix = proj
        args += [o, wo, gmix]
        specs += [row, _full(wo.shape), _full((1, d))]
    args += [gpre, gpost, wg, wu, wd]
    specs += [_full((1, d)), _full((1, d)), _full(wg.shape), _full(wu.shape), _full(wd.shape)]
    return pl.pallas_call(
        functools.partial(_ffn_kernel, ff_chunk=512, has_proj=proj is not None),
        out_shape=jax.ShapeDtypeStruct((t, d), F32),
        grid=(t // tm,),
        in_specs=specs,
        out_specs=row,
        compiler_params=_params("parallel"),
        name="ffn_proj" if proj is not None else "ffn",
    )(*args)


def _qkv_kernel(x_ref, gpre_ref, w_ref, k_ref, v_ref, qkv_ref):
    d = x_ref.shape[1]
    h = _rms(x_ref[...], gpre_ref[...]).astype(BF16)
    qkv = _dot(h, w_ref[...])
    k_ref[...] = qkv[:, d:2 * d]
    v_ref[...] = qkv[:, 2 * d:]
    qkv_ref[:, :d] = (qkv[:, :d] * HEAD_DIM ** -0.5).astype(BF16)
    qkv_ref[:, d:] = qkv[:, d:].astype(BF16)


def _qkv(x, gpre, w_qkv, *, tm):
    t, d = x.shape
    row = pl.BlockSpec((tm, d), lambda i: (i, 0))
    return pl.pallas_call(
        _qkv_kernel,
        out_shape=[jax.ShapeDtypeStruct((t, d), F32), jax.ShapeDtypeStruct((t, d), F32),
                   jax.ShapeDtypeStruct((t, 3 * d), BF16)],
        grid=(t // tm,),
        in_specs=[row, _full((1, d)), _full(w_qkv.shape)],
        out_specs=[row, row, pl.BlockSpec((tm, 3 * d), lambda i: (i, 0))],
        compiler_params=_params("parallel"),
        name="qkv",
    )(x, gpre, w_qkv)


def _suffix_matrix(tk):
    j = lax.broadcasted_iota(jnp.int32, (tk, tk + LANES), 0)
    s = lax.broadcasted_iota(jnp.int32, (tk, tk + LANES), 1)
    return jnp.where((j > s) | (s >= tk), 1.0, 0.0).astype(BF16)


def _sb_block(qh, k, v, later, acc, suffix, causal):
    tk = k.shape[0]
    z = lax.dot_general(qh, k, (((1,), (1,)), ((), ())), preferred_element_type=F32)
    sp = jnp.maximum(z, 0.0) + jnp.log(1.0 + jnp.exp(-jnp.abs(z)))
    if causal is not None:
        sp = jnp.where(causal, sp, 0.0)
    hi = sp.astype(BF16)
    lo = (sp - hi.astype(F32)).astype(BF16)
    cs = _dot(hi, suffix) + _dot(lo, suffix)
    arg = z - sp - cs[:, :tk] - jnp.concatenate([later] * (tk // LANES), axis=1)
    a = jnp.exp(arg)
    if causal is not None:
        a = jnp.where(causal, a, 0.0)
    acc = acc + _dot(a.astype(BF16), v)
    return later + cs[:, tk:], acc


def _sb_heads(q2, diag_kv, old_kv, n_old, tq, tk):
    suffix = _suffix_matrix(tk)
    lane = lax.broadcasted_iota(jnp.int32, (tq, LANES), 1)
    first = lane < HEAD_DIM
    row = lax.broadcasted_iota(jnp.int32, (tq, tk), 0)
    col = lax.broadcasted_iota(jnp.int32, (tq, tk), 1)
    causal = col < row
    zeros = jnp.zeros((tq, LANES), F32)
    outs = []
    for sel in (first, jnp.logical_not(first)):
        qh = jnp.where(sel, q2, jnp.zeros_like(q2))
        kd, vd = diag_kv()
        carry = _sb_block(qh, kd, vd, zeros, zeros, suffix, causal)

        def body(i, c):
            k, v = old_kv(n_old - 1 - i)
            return _sb_block(qh, k, v, c[0], c[1], suffix, None)

        carry = lax.fori_loop(0, n_old, body, carry)
        outs.append(carry[1])
    return jnp.where(first, outs[0], outs[1])


def _attn_prompt_kernel(q_ref, k_ref, v_ref, o_ref, *, blk):
    qi = pl.program_id(2)

    def kv(kb):
        off = pl.multiple_of(kb * blk, blk)
        return k_ref[pl.ds(off, blk), :], v_ref[pl.ds(off, blk), :]

    o = _sb_heads(q_ref[...], lambda: kv(qi), kv, qi, blk, blk)
    o_ref[...] = o.astype(BF16)


def _attn_prompt(qkv, *, batch, seq):
    t = qkv.shape[0]
    d = qkv.shape[1] // 3
    blk = ATTN_BLOCK
    nq, npair = seq // blk, d // LANES
    return pl.pallas_call(
        functools.partial(_attn_prompt_kernel, blk=blk),
        out_shape=jax.ShapeDtypeStruct((t, d), BF16),
        grid=(batch, npair, nq),
        in_specs=[
            pl.BlockSpec((blk, LANES), lambda b, p, i: (b * nq + i, p)),
            pl.BlockSpec((seq, LANES), lambda b, p, i: (b, npair + p)),
            pl.BlockSpec((seq, LANES), lambda b, p, i: (b, 2 * npair + p)),
        ],
        out_specs=pl.BlockSpec((blk, LANES), lambda b, p, i: (b * nq + i, p)),
        compiler_params=_params("parallel", "parallel", "arbitrary"),
        name="attn_prompt",
    )(qkv, qkv, qkv)


def _attn_sample_kernel(q_ref, kn_ref, vn_ref, kc_ref, vc_ref, o_ref, *, blk, n_old):
    tq = q_ref.shape[0]
    pad = jnp.zeros((blk - tq, LANES), BF16)

    def diag():
        return (jnp.concatenate([kn_ref[...], pad], axis=0), jnp.concatenate([vn_ref[...], pad], axis=0))

    def old(kb):
        off = pl.multiple_of(kb * blk, blk)
        return kc_ref[pl.ds(off, blk), :].astype(BF16), vc_ref[pl.ds(off, blk), :].astype(BF16)

    o = _sb_heads(q_ref[...], diag, old, n_old, tq, blk)
    o_ref[...] = o.astype(BF16)


def _attn_sample(qkv, cache_k, cache_v, *, batch, seq, past):
    t = qkv.shape[0]
    d = qkv.shape[1] // 3
    blk = ATTN_BLOCK
    npair = d // LANES
    return pl.pallas_call(
        functools.partial(_attn_sample_kernel, blk=blk, n_old=past // blk),
        out_shape=jax.ShapeDtypeStruct((t, d), BF16),
        grid=(batch, npair),
        in_specs=[
            pl.BlockSpec((seq, LANES), lambda b, p: (b, p)),
            pl.BlockSpec((seq, LANES), lambda b, p: (b, npair + p)),
            pl.BlockSpec((seq, LANES), lambda b, p: (b, 2 * npair + p)),
            pl.BlockSpec((past, LANES), lambda b, p: (b, p)),
            pl.BlockSpec((past, LANES), lambda b, p: (b, p)),
        ],
        out_specs=pl.BlockSpec((seq, LANES), lambda b, p: (b, p)),
        compiler_params=_params("parallel", "parallel"),
        name="attn_sample",
    )(qkv, qkv, qkv, cache_k, cache_v)


def _trunk(x, prev_pool, cache_kv, w, *, is_prompt, tm_mix, tm_tok):
    batch, seq, d = x.shape
    t = batch * seq
    x = x.reshape(t, d)
    prev = jnp.pad(prev_pool, ((0, 0), (POOL_HALO - prev_pool.shape[1], 0), (0, 0)))
    chunk = 2 * CHUNK if is_prompt else seq
    outs = _mixer_ab(x, prev, w["norm_mix_pre"][0], w["norm_mix_post"][0], w["w_in_ab"], w["ln_v_g"],
                     w["ln_v_b"], w["w_spatial"], w["b_spatial"][:, :chunk].T, w["w_pool_map"],
                     w["pool_scale"], w["w_out_ab"], batch=batch, tm=tm_mix, chunk=chunk,
                     start_pos=0 if is_prompt else cache_kv[0].shape[1], emit_vn=not is_prompt)
    x, tail = outs[0], outs[1]
    x = _ffn(x, w["norm_ffn_pre"][0], w["norm_ffn_post"][0], w["w_gate"][0], w["w_up"][0], w["w_down"][0],
             tm=tm_tok)
    k, v, qkv = _qkv(x, w["norm_mix_pre"][1], w["w_qkv"], tm=tm_tok)
    if is_prompt:
        o = _attn_prompt(qkv, batch=batch, seq=seq)
    else:
        past = cache_kv[0].shape[1]
        o = _attn_sample(qkv, cache_kv[0].reshape(batch * past, d), cache_kv[1].reshape(batch * past, d),
                         batch=batch, seq=seq, past=past)
    x = _ffn(x, w["norm_ffn_pre"][1], w["norm_ffn_post"][1], w["w_gate"][1], w["w_up"][1], w["w_down"][1],
             tm=tm_tok, proj=(o, w["w_o_sb"], w["norm_mix_post"][1]))
    heads = d // HEAD_DIM
    res = [x.reshape(batch, seq, d), tail[None, :, 1:, :],
           k.reshape(1, batch, seq, heads, HEAD_DIM), v.reshape(1, batch, seq, heads, HEAD_DIM)]
    if not is_prompt:
        res.append(outs[2].reshape(1, batch, seq, -1))
    return res


def kernel(x_prompt, x_sample, state_pool, cache_k, cache_v, norm_mix_pre, norm_mix_post, norm_ffn_pre,
           norm_ffn_post, w_in_ab, ln_v_g, ln_v_b, w_spatial, b_spatial, w_pool_map, pool_scale, w_out_ab,
           w_qkv, w_o_sb, w_gate, w_up, w_down):
    depth, d = norm_mix_pre.shape
    assert depth == 2 and w_in_ab.shape[0] == 1 and w_qkv.shape[0] == 1
    w = dict(
        norm_mix_pre=norm_mix_pre.reshape(depth, 1, d), norm_mix_post=norm_mix_post.reshape(depth, 1, d),
        norm_ffn_pre=norm_ffn_pre.reshape(depth, 1, d), norm_ffn_post=norm_ffn_post.reshape(depth, 1, d),
        w_in_ab=w_in_ab[0].astype(BF16), ln_v_g=ln_v_g, ln_v_b=ln_v_b, w_spatial=w_spatial[0],
        b_spatial=b_spatial[0], w_pool_map=w_pool_map[0].astype(BF16), pool_scale=pool_scale,
        w_out_ab=w_out_ab[0].astype(BF16), w_qkv=w_qkv[0].astype(BF16), w_o_sb=w_o_sb[0].astype(BF16),
        w_gate=w_gate.astype(BF16), w_up=w_up.astype(BF16), w_down=w_down.astype(BF16),
    )
    batch = x_prompt.shape[0]
    zero_pool = jnp.zeros((batch,) + state_pool.shape[2:], F32)
    y_p, pool_p, k_p, v_p = _trunk(x_prompt, zero_pool, None, w, is_prompt=True, tm_mix=256, tm_tok=512)
    y_s, pool_s, k_s, v_s, vn_s = _trunk(x_sample, state_pool[0], (cache_k[0], cache_v[0]), w,
                                         is_prompt=False, tm_mix=x_sample.shape[1],
                                         tm_tok=x_sample.shape[0] * x_sample.shape[1])
    return (y_p, y_s, pool_p, k_p, v_p, pool_s, k_s, v_s, vn_s)
```

```python
import functools

import jax
import jax.numpy as jnp
from jax import lax
from jax.experimental import pallas as pl
from jax.experimental.pallas import tpu as pltpu

F32 = jnp.float32
BF16 = jnp.bfloat16

EPS = 1e-6
CHUNK = 64
POOL_WINDOWS = (2, 4, 8, 16)
POOL_HALO = 16
LANES = 128
HEAD_DIM = 64
ATTN_BLOCK = 256
VMEM_LIMIT = 56 * 1024 * 1024


def _rms(x, g):
    return x * lax.rsqrt(jnp.mean(x * x, axis=-1, keepdims=True) + EPS) * g


def _dot(a, b):
    return jnp.dot(a, b, preferred_element_type=F32)


def _params(*sem):
    return pltpu.CompilerParams(dimension_semantics=sem, vmem_limit_bytes=VMEM_LIMIT)


def _full(shape):
    return pl.BlockSpec(shape, lambda *_: (0,) * len(shape))


def _mixer_ab_kernel(x_ref, prev_ref, gpre_ref, gpost_ref, win_ref, lng_ref, lnb_ref, ws_ref,
                     bs_ref, wmap_ref, pscale_ref, wout_ref, *rest,
                     tm, chunk, tiles_per_batch, start_pos, emit_vn):
    if emit_vn:
        x1_ref, tail_ref, vn_ref, ext_ref, cat_ref = rest
    else:
        x1_ref, tail_ref, ext_ref, cat_ref = rest
    aw = 4 * LANES
    tib = pl.program_id(0) % tiles_per_batch

    @pl.when(tib == 0)
    def _():
        ext_ref[0:POOL_HALO, :] = prev_ref[...]

    x = x_ref[...]
    h = _rms(x, gpre_ref[...]).astype(BF16)
    z = _dot(h, win_ref[...])
    zuv = z[:, :2 * aw]
    uv = 0.5 * zuv * (1.0 + lax.erf(zuv * (0.5 ** 0.5)))
    u, v = uv[:, :aw], uv[:, aw:]
    mu = jnp.mean(v, axis=-1, keepdims=True)
    vc = v - mu
    vn = vc * lax.rsqrt(jnp.mean(vc * vc, axis=-1, keepdims=True) + EPS) * lng_ref[...] + lnb_ref[...]
    if emit_vn:
        vn_ref[...] = vn

    ri = lax.broadcasted_iota(jnp.int32, (chunk, LANES), 0)
    ci = lax.broadcasted_iota(jnp.int32, (chunk, LANES), 1)
    allowed = (ci // CHUNK <= ri // CHUNK) & (ci < chunk)
    bias = bs_ref[...]
    vnb = vn.astype(BF16)
    for g in range(4):
        wg = jnp.where(allowed, ws_ref[g, 0:chunk, :], 0.0).astype(BF16)
        bg = bias[:, g:g + 1]
        for c in range(tm // chunk):
            vblk = vnb[c * chunk:(c + 1) * chunk, g * LANES:(g + 1) * LANES]
            if chunk < LANES:
                vblk = jnp.concatenate([vblk, jnp.zeros((LANES - chunk, LANES), BF16)], axis=0)
            s = _dot(wg, vblk) + bg
            ublk = u[c * chunk:(c + 1) * chunk, g * LANES:(g + 1) * LANES]
            cat_ref[c * chunk:(c + 1) * chunk, g * LANES:(g + 1) * LANES] = (ublk * s).astype(BF16)

    p = z[:, 2 * aw:]
    ext_ref[POOL_HALO:POOL_HALO + tm, :] = p
    e1 = ext_ref[...]
    e2 = e1 + pltpu.roll(e1, 1, axis=0)
    e4 = e2[:, LANES:] + pltpu.roll(e2[:, LANES:], 2, axis=0)
    e8 = e4[:, LANES:] + pltpu.roll(e4[:, LANES:], 4, axis=0)
    e16 = e8[:, LANES:] + pltpu.roll(e8[:, LANES:], 8, axis=0)
    wins = (e2[POOL_HALO:, :LANES], e4[POOL_HALO:, :LANES], e8[POOL_HALO:, :LANES], e16[POOL_HALO:, :])
    pos = start_pos + tib * tm + lax.broadcasted_iota(jnp.int32, (tm, 1), 0)
    pscale = pscale_ref[...]
    for g, w in enumerate(POOL_WINDOWS):
        inv_cnt = 1.0 / jnp.minimum(w, pos + 1).astype(F32)
        d = wins[g] * inv_cnt - p[:, g * LANES:(g + 1) * LANES]
        yb = _dot(d.astype(BF16), wmap_ref[g]) * pscale[:, g * LANES:(g + 1) * LANES]
        cat_ref[:, aw + g * LANES:aw + (g + 1) * LANES] = yb.astype(BF16)

    tail = ext_ref[tm:tm + POOL_HALO, :]
    ext_ref[0:POOL_HALO, :] = tail

    @pl.when(tib == tiles_per_batch - 1)
    def _():
        tail_ref[...] = tail

    y = _dot(cat_ref[...], wout_ref[...])
    x1_ref[...] = x + _rms(y, gpost_ref[...])


def _mixer_ab(x, prev, gpre, gpost, w_in, ln_g, ln_b, w_s, b_s, w_map, p_scale, w_out,
              *, batch, tm, chunk, start_pos, emit_vn):
    t, d = x.shape
    aw = 4 * LANES
    tiles_per_batch = t // batch // tm
    kern = functools.partial(_mixer_ab_kernel, tm=tm, chunk=chunk, tiles_per_batch=tiles_per_batch,
                             start_pos=start_pos, emit_vn=emit_vn)
    out_shape = [jax.ShapeDtypeStruct((t, d), F32), jax.ShapeDtypeStruct((batch, POOL_HALO, aw), F32)]
    out_specs = [pl.BlockSpec((tm, d), lambda i: (i, 0)),
                 pl.BlockSpec((None, POOL_HALO, aw), lambda i: (i // tiles_per_batch, 0, 0))]
    if emit_vn:
        out_shape.append(jax.ShapeDtypeStruct((t, aw), F32))
        out_specs.append(pl.BlockSpec((tm, aw), lambda i: (i, 0)))
    return pl.pallas_call(
        kern,
        out_shape=out_shape,
        grid=(t // tm,),
        in_specs=[
            pl.BlockSpec((tm, d), lambda i: (i, 0)),
            pl.BlockSpec((None, POOL_HALO, aw), lambda i: (i // tiles_per_batch, 0, 0)),
            _full((1, d)), _full((1, d)), _full(w_in.shape), _full((1, aw)), _full((1, aw)),
            _full(w_s.shape), _full(b_s.shape), _full(w_map.shape), _full((1, aw)), _full(w_out.shape),
        ],
        out_specs=out_specs,
        scratch_shapes=[pltpu.VMEM((tm + POOL_HALO, aw), F32), pltpu.VMEM((tm, 2 * aw), BF16)],
        compiler_params=_params("arbitrary"),
        name="mixer_ab",
    )(x, prev, gpre, gpost, w_in, ln_g, ln_b, w_s, b_s, w_map, p_scale, w_out)


def _ffn_kernel(*refs, ff_chunk, has_proj):
    if has_proj:
        x_ref, o_ref, wo_ref, gmix_ref, gpre_ref, gpost_ref, wg_ref, wu_ref, wd_ref, out_ref = refs
        x = x_ref[...] + _rms(_dot(o_ref[...], wo_ref[...]), gmix_ref[...])
    else:
        x_ref, gpre_ref, gpost_ref, wg_ref, wu_ref, wd_ref, out_ref = refs
        x = x_ref[...]
    h = _rms(x, gpre_ref[...]).astype(BF16)
    d_ff = wg_ref.shape[1]
    f = None
    for c0 in range(0, d_ff, ff_chunk):
        c1 = min(c0 + ff_chunk, d_ff)
        act = jax.nn.silu(_dot(h, wg_ref[:, c0:c1])) * _dot(h, wu_ref[:, c0:c1])
        part = _dot(act.astype(BF16), wd_ref[c0:c1, :])
        f = part if f is None else f + part
    out_ref[...] = x + _rms(f, gpost_ref[...])


def _ffn(x, gpre, gpost, wg, wu, wd, *, tm, proj=None):
    t, d = x.shape
    row = pl.BlockSpec((tm, d), lambda i: (i, 0))
    args, specs = [x], [row]
    if proj is not None:
        o, wo, gmix = proj
        args += [o, wo, gmix]
        specs += [row, _full(wo.shape), _full((1, d))]
    args += [gpre, gpost, wg, wu, wd]
    specs += [_full((1, d)), _full((1, d)), _full(wg.shape), _full(wu.shape), _full(wd.shape)]
    return pl.pallas_call(
        functools.partial(_ffn_kernel, ff_chunk=512, has_proj=proj is not None),
        out_shape=jax.ShapeDtypeStruct((t, d), F32),
        grid=(t // tm,),
        in_specs=specs,
        out_specs=row,
        compiler_params=_params("parallel"),
        name="ffn_proj" if proj is not None else "ffn",
    )(*args)


Q_SCALE = HEAD_DIM ** -0.5 * 1.4426950408889634


def _qkv_rows_kernel(x_ref, gpre_ref, w_ref, k_ref, v_ref, qkv_ref):
    d = x_ref.shape[1]
    h = _rms(x_ref[...], gpre_ref[...]).astype(BF16)
    qkv = _dot(h, w_ref[...])
    k_ref[...] = qkv[:, d:2 * d]
    v_ref[...] = qkv[:, 2 * d:]
    qkv_ref[:, :d] = (qkv[:, :d] * Q_SCALE).astype(BF16)
    qkv_ref[:, d:] = qkv[:, d:].astype(BF16)


def _qkv_rows(x, gpre, w_qkv, *, tm):
    t, d = x.shape
    row = pl.BlockSpec((tm, d), lambda i: (i, 0))
    return pl.pallas_call(
        _qkv_rows_kernel,
        out_shape=[jax.ShapeDtypeStruct((t, d), F32), jax.ShapeDtypeStruct((t, d), F32),
                   jax.ShapeDtypeStruct((t, 3 * d), BF16)],
        grid=(t // tm,),
        in_specs=[row, _full((1, d)), _full(w_qkv.shape)],
        out_specs=[row, row, pl.BlockSpec((tm, 3 * d), lambda i: (i, 0))],
        compiler_params=_params("parallel"),
        name="qkv_rows",
    )(x, gpre, w_qkv)


def _qkv_cols_kernel(x_ref, gpre_ref, wq_ref, wkt_ref, wvt_ref, q_ref, kt_ref, vt_ref, ktb_ref, vtb_ref):
    h = _rms(x_ref[...], gpre_ref[...]).astype(BF16)
    q_ref[...] = (_dot(h, wq_ref[...]) * Q_SCALE).astype(BF16)
    nt = (((1,), (1,)), ((), ()))
    kt = lax.dot_general(wkt_ref[...], h, nt, preferred_element_type=F32)
    kt_ref[...] = kt
    ktb_ref[...] = kt.astype(BF16)
    vt = lax.dot_general(wvt_ref[...], h, nt, preferred_element_type=F32)
    vt_ref[...] = vt
    vtb_ref[...] = vt.astype(BF16)


def _qkv_cols(x, gpre, wq, wkt, wvt, *, batch, tm):
    t, d = x.shape
    seq = t // batch
    per = seq // tm
    row = pl.BlockSpec((tm, d), lambda i: (i, 0))
    col = pl.BlockSpec((None, d, tm), lambda i: (i // per, 0, i % per))
    return pl.pallas_call(
        _qkv_cols_kernel,
        out_shape=[jax.ShapeDtypeStruct((t, d), BF16),
                   jax.ShapeDtypeStruct((batch, d, seq), F32), jax.ShapeDtypeStruct((batch, d, seq), F32),
                   jax.ShapeDtypeStruct((batch, d, seq), BF16), jax.ShapeDtypeStruct((batch, d, seq), BF16)],
        grid=(t // tm,),
        in_specs=[row, _full((1, d)), _full(wq.shape), _full(wkt.shape), _full(wvt.shape)],
        out_specs=[row, col, col, col, col],
        compiler_params=_params("parallel"),
        name="qkv_cols",
    )(x, gpre, wq, wkt, wvt)


def _suffix_matrix(tk):
    j = lax.broadcasted_iota(jnp.int32, (tk, tk), 0)
    s = lax.broadcasted_iota(jnp.int32, (tk, tk), 1)
    return jnp.where(j > s, 1.0, 0.0).astype(BF16)


def _sb_block(qs, k, v, later_ref, acc_ref, suffix, causal, transposed):
    nt = (((1,), (1,)), ((), ()))
    n = len(qs)
    if transposed:
        zs = [_dot(q, k) for q in qs]
    else:
        zs = [lax.dot_general(q, k, nt, preferred_element_type=F32) for q in qs]
    tk = zs[0].shape[1]
    sps = [jnp.maximum(z, 0.0) + jnp.log2(1.0 + jnp.exp2(-jnp.abs(z))) for z in zs]
    if causal is not None:
        sps = [jnp.where(causal, sp, 0.0) for sp in sps]
    his = [sp.astype(BF16) for sp in sps]
    los = [(sp - hi.astype(F32)).astype(BF16) for sp, hi in zip(sps, his)]
    css = [_dot(hi, suffix) + _dot(lo, suffix) for hi, lo in zip(his, los)]
    laters = [later_ref[c] for c in range(n)]
    aa = [jnp.exp2(z - sp - cs - jnp.concatenate([lt] * (tk // LANES), axis=1))
          for z, sp, cs, lt in zip(zs, sps, css, laters)]
    if causal is not None:
        aa = [jnp.where(causal, a, 0.0) for a in aa]
    aa = [a.astype(BF16) for a in aa]
    if transposed:
        pv = [lax.dot_general(a, v, nt, preferred_element_type=F32) for a in aa]
    else:
        pv = [_dot(a, v) for a in aa]
    for c in range(n):
        acc_ref[c] += pv[c]
        later_ref[c] = laters[c] + jnp.broadcast_to(css[c][:, 0:1] + sps[c][:, 0:1], laters[c].shape)


def _sb_head_pair(q2, diag_kv, diag_transposed, old_kv, n_old, tk, later_ref, acc_ref):
    tq = q2.shape[0]
    suffix = _suffix_matrix(tk)
    first = lax.broadcasted_iota(jnp.int32, (tq, LANES), 1) < HEAD_DIM
    zero = jnp.zeros_like(q2)
    qs = (jnp.where(first, q2, zero), jnp.where(first, zero, q2))
    row = lax.broadcasted_iota(jnp.int32, (tq, tk), 0)
    col = lax.broadcasted_iota(jnp.int32, (tq, tk), 1)
    causal = col < row
    later_ref[...] = jnp.zeros_like(later_ref)
    acc_ref[...] = jnp.zeros_like(acc_ref)
    kd, vd = diag_kv()
    _sb_block(qs, kd, vd, later_ref, acc_ref, suffix, causal, diag_transposed)

    @pl.loop(0, n_old)
    def _(i):
        k, v = old_kv(n_old - 1 - i)
        _sb_block(qs, k, v, later_ref, acc_ref, suffix, None, True)

    return jnp.where(first, acc_ref[0], acc_ref[1])


def _attn_prompt_kernel(q_ref, kt_ref, vt_ref, o_ref, later_ref, acc_ref, *, blk):
    qi = pl.program_id(2)

    def kv(kb):
        off = pl.multiple_of(kb * blk, blk)
        return kt_ref[:, pl.ds(off, blk)], vt_ref[:, pl.ds(off, blk)]

    o = _sb_head_pair(q_ref[...], lambda: kv(qi), True, kv, qi, blk, later_ref, acc_ref)
    o_ref[...] = o.astype(BF16)


def _attn_prompt(q, kt, vt):
    t, d = q.shape
    batch, _, seq = kt.shape
    blk = ATTN_BLOCK
    nq = seq // blk
    pair = pl.BlockSpec((None, LANES, seq), lambda b, p, i: (b, p, 0))
    qblk = pl.BlockSpec((blk, LANES), lambda b, p, i: (b * nq + i, p))
    return pl.pallas_call(
        functools.partial(_attn_prompt_kernel, blk=blk),
        out_shape=jax.ShapeDtypeStruct((t, d), BF16),
        grid=(batch, d // LANES, nq),
        in_specs=[qblk, pair, pair],
        out_specs=qblk,
        scratch_shapes=[pltpu.VMEM((2, blk, LANES), F32), pltpu.VMEM((2, blk, LANES), F32)],
        compiler_params=_params("parallel", "parallel", "arbitrary"),
        name="attn_prompt",
    )(q, kt, vt)


def _attn_sample_kernel(q_ref, kn_ref, vn_ref, kct_ref, vct_ref, o_ref, later_ref, acc_ref, *, blk, n_old):
    tq = q_ref.shape[0]
    pad = jnp.zeros((blk - tq, LANES), BF16)

    def diag():
        return (jnp.concatenate([kn_ref[...], pad], axis=0), jnp.concatenate([vn_ref[...], pad], axis=0))

    def old(kb):
        off = pl.multiple_of(kb * blk, blk)
        return kct_ref[:, pl.ds(off, blk)].astype(BF16), vct_ref[:, pl.ds(off, blk)].astype(BF16)

    o = _sb_head_pair(q_ref[...], diag, False, old, n_old, blk, later_ref, acc_ref)
    o_ref[...] = o.astype(BF16)


def _attn_sample(qkv, cache_kt, cache_vt, *, seq):
    t = qkv.shape[0]
    d = qkv.shape[1] // 3
    batch, _, past = cache_kt.shape
    blk = ATTN_BLOCK
    npair = d // LANES
    pair = pl.BlockSpec((None, LANES, past), lambda b, p: (b, p, 0))
    return pl.pallas_call(
        functools.partial(_attn_sample_kernel, blk=blk, n_old=past // blk),
        out_shape=jax.ShapeDtypeStruct((t, d), BF16),
        grid=(batch, npair),
        in_specs=[
            pl.BlockSpec((seq, LANES), lambda b, p: (b, p)),
            pl.BlockSpec((seq, LANES), lambda b, p: (b, npair + p)),
            pl.BlockSpec((seq, LANES), lambda b, p: (b, 2 * npair + p)),
            pair, pair,
        ],
        out_specs=pl.BlockSpec((seq, LANES), lambda b, p: (b, p)),
        scratch_shapes=[pltpu.VMEM((2, seq, LANES), F32), pltpu.VMEM((2, seq, LANES), F32)],
        compiler_params=_params("parallel", "parallel"),
        name="attn_sample",
    )(qkv, qkv, qkv, cache_kt, cache_vt)


def _features_first(kv):
    b, s, h, e = kv.shape
    return jnp.transpose(kv, (0, 2, 3, 1)).reshape(b, h * e, s)


def _tokens_first(kvt, heads):
    b, d, s = kvt.shape
    return jnp.transpose(kvt.reshape(b, heads, d // heads, s), (0, 3, 1, 2))[None]


def _trunk(x, prev_pool, cache_kv, w, *, is_prompt, tm_mix, tm_tok):
    batch, seq, d = x.shape
    heads = d // HEAD_DIM
    t = batch * seq
    x = x.reshape(t, d)
    prev = jnp.pad(prev_pool, ((0, 0), (POOL_HALO - prev_pool.shape[1], 0), (0, 0)))
    chunk = 2 * CHUNK if is_prompt else seq
    outs = _mixer_ab(x, prev, w["norm_mix_pre"][0], w["norm_mix_post"][0], w["w_in_ab"], w["ln_v_g"],
                     w["ln_v_b"], w["w_spatial"], w["b_spatial"][:, :chunk].T, w["w_pool_map"],
                     w["pool_scale"], w["w_out_ab"], batch=batch, tm=tm_mix, chunk=chunk,
                     start_pos=0 if is_prompt else cache_kv[0].shape[1], emit_vn=not is_prompt)
    x, tail = outs[0], outs[1]
    x = _ffn(x, w["norm_ffn_pre"][0], w["norm_ffn_post"][0], w["w_gate"][0], w["w_up"][0], w["w_down"][0],
             tm=tm_tok)
    if is_prompt:
        q, kt, vt, ktb, vtb = _qkv_cols(x, w["norm_mix_pre"][1], w["w_q"], w["w_kt"], w["w_vt"],
                                        batch=batch, tm=tm_tok)
        o = _attn_prompt(q, ktb, vtb)
        k, v = _tokens_first(kt, heads), _tokens_first(vt, heads)
    else:
        k, v, qkv = _qkv_rows(x, w["norm_mix_pre"][1], w["w_qkv"], tm=tm_tok)
        o = _attn_sample(qkv, _features_first(cache_kv[0]), _features_first(cache_kv[1]), seq=seq)
        k, v = k.reshape(1, batch, seq, heads, HEAD_DIM), v.reshape(1, batch, seq, heads, HEAD_DIM)
    x = _ffn(x, w["norm_ffn_pre"][1], w["norm_ffn_post"][1], w["w_gate"][1], w["w_up"][1], w["w_down"][1],
             tm=tm_tok, proj=(o, w["w_o_sb"], w["norm_mix_post"][1]))
    res = [x.reshape(batch, seq, d), tail[None, :, 1:, :], k, v]
    if not is_prompt:
        res.append(outs[2].reshape(1, batch, seq, -1))
    return res


def kernel(x_prompt, x_sample, state_pool, cache_k, cache_v, norm_mix_pre, norm_mix_post, norm_ffn_pre,
           norm_ffn_post, w_in_ab, ln_v_g, ln_v_b, w_spatial, b_spatial, w_pool_map, pool_scale, w_out_ab,
           w_qkv, w_o_sb, w_gate, w_up, w_down):
    depth, d = norm_mix_pre.shape
    assert depth == 2 and w_in_ab.shape[0] == 1 and w_qkv.shape[0] == 1
    wqkv = w_qkv[0].astype(BF16)
    w = dict(
        norm_mix_pre=norm_mix_pre.reshape(depth, 1, d), norm_mix_post=norm_mix_post.reshape(depth, 1, d),
        norm_ffn_pre=norm_ffn_pre.reshape(depth, 1, d), norm_ffn_post=norm_ffn_post.reshape(depth, 1, d),
        w_in_ab=w_in_ab[0].astype(BF16), ln_v_g=ln_v_g, ln_v_b=ln_v_b, w_spatial=w_spatial[0],
        b_spatial=b_spatial[0], w_pool_map=w_pool_map[0].astype(BF16), pool_scale=pool_scale,
        w_out_ab=w_out_ab[0].astype(BF16), w_qkv=wqkv, w_q=wqkv[:, :d], w_kt=wqkv[:, d:2 * d].T,
        w_vt=wqkv[:, 2 * d:].T, w_o_sb=w_o_sb[0].astype(BF16),
        w_gate=w_gate.astype(BF16), w_up=w_up.astype(BF16), w_down=w_down.astype(BF16),
    )
    batch = x_prompt.shape[0]
    zero_pool = jnp.zeros((batch,) + state_pool.shape[2:], F32)
    y_p, pool_p, k_p, v_p = _trunk(x_prompt, zero_pool, None, w, is_prompt=True, tm_mix=256, tm_tok=512)
    y_s, pool_s, k_s, v_s, vn_s = _trunk(x_sample, state_pool[0], (cache_k[0], cache_v[0]), w,
                                         is_prompt=False, tm_mix=x_sample.shape[1],
                                         tm_tok=x_sample.shape[0] * x_sample.shape[1])
    return (y_p, y_s, pool_p, k_p, v_p, pool_s, k_s, v_s, vn_s)
```

```python
import functools

import jax
import jax.numpy as jnp
from jax import lax
from jax.experimental import pallas as pl
from jax.experimental.pallas import tpu as pltpu

F32 = jnp.float32
BF16 = jnp.bfloat16

EPS = 1e-6
CHUNK = 64
POOL_WINDOWS = (2, 4, 8, 16)
POOL_HALO = 16
LANES = 128
HEAD_DIM = 64
ATTN_BLOCK = 256
VMEM_LIMIT = 56 * 1024 * 1024


def _rms(x, g):
    return x * lax.rsqrt(jnp.mean(x * x, axis=-1, keepdims=True) + EPS) * g


def _dot(a, b):
    return jnp.dot(a, b, preferred_element_type=F32)


def _params(*sem):
    return pltpu.CompilerParams(dimension_semantics=sem, vmem_limit_bytes=VMEM_LIMIT)


def _full(shape):
    return pl.BlockSpec(shape, lambda *_: (0,) * len(shape))


def _mixer_ab_kernel(x_ref, prev_ref, gpre_ref, gpost_ref, win_ref, lng_ref, lnb_ref, ws_ref,
                     bs_ref, wmap_ref, pscale_ref, wout_ref, *rest,
                     tm, chunk, tiles_per_batch, start_pos, emit_vn):
    if emit_vn:
        x1_ref, tail_ref, vn_ref, ext_ref, cat_ref = rest
    else:
        x1_ref, tail_ref, ext_ref, cat_ref = rest
    aw = 4 * LANES
    tib = pl.program_id(0) % tiles_per_batch

    @pl.when(tib == 0)
    def _():
        ext_ref[0:POOL_HALO, :] = prev_ref[...]

    x = x_ref[...]
    h = _rms(x, gpre_ref[...]).astype(BF16)
    z = _dot(h, win_ref[...])
    zuv = z[:, :2 * aw]
    uv = 0.5 * zuv * (1.0 + lax.erf(zuv * (0.5 ** 0.5)))
    u, v = uv[:, :aw], uv[:, aw:]
    mu = jnp.mean(v, axis=-1, keepdims=True)
    vc = v - mu
    vn = vc * lax.rsqrt(jnp.mean(vc * vc, axis=-1, keepdims=True) + EPS) * lng_ref[...] + lnb_ref[...]
    if emit_vn:
        vn_ref[...] = vn

    ri = lax.broadcasted_iota(jnp.int32, (chunk, LANES), 0)
    ci = lax.broadcasted_iota(jnp.int32, (chunk, LANES), 1)
    allowed = (ci // CHUNK <= ri // CHUNK) & (ci < chunk)
    bias = bs_ref[...]
    vnb = vn.astype(BF16)
    for g in range(4):
        wg = jnp.where(allowed, ws_ref[g, 0:chunk, :], 0.0).astype(BF16)
        bg = bias[:, g:g + 1]
        for c in range(tm // chunk):
            vblk = vnb[c * chunk:(c + 1) * chunk, g * LANES:(g + 1) * LANES]
            if chunk < LANES:
                vblk = jnp.concatenate([vblk, jnp.zeros((LANES - chunk, LANES), BF16)], axis=0)
            s = _dot(wg, vblk) + bg
            ublk = u[c * chunk:(c + 1) * chunk, g * LANES:(g + 1) * LANES]
            cat_ref[c * chunk:(c + 1) * chunk, g * LANES:(g + 1) * LANES] = (ublk * s).astype(BF16)

    p = z[:, 2 * aw:]
    ext_ref[POOL_HALO:POOL_HALO + tm, :] = p
    e1 = ext_ref[...]
    e2 = e1 + pltpu.roll(e1, 1, axis=0)
    e4 = e2[:, LANES:] + pltpu.roll(e2[:, LANES:], 2, axis=0)
    e8 = e4[:, LANES:] + pltpu.roll(e4[:, LANES:], 4, axis=0)
    e16 = e8[:, LANES:] + pltpu.roll(e8[:, LANES:], 8, axis=0)
    wins = (e2[POOL_HALO:, :LANES], e4[POOL_HALO:, :LANES], e8[POOL_HALO:, :LANES], e16[POOL_HALO:, :])
    pos = start_pos + tib * tm + lax.broadcasted_iota(jnp.int32, (tm, 1), 0)
    pscale = pscale_ref[...]
    for g, w in enumerate(POOL_WINDOWS):
        inv_cnt = 1.0 / jnp.minimum(w, pos + 1).astype(F32)
        d = wins[g] * inv_cnt - p[:, g * LANES:(g + 1) * LANES]
        yb = _dot(d.astype(BF16), wmap_ref[g]) * pscale[:, g * LANES:(g + 1) * LANES]
        cat_ref[:, aw + g * LANES:aw + (g + 1) * LANES] = yb.astype(BF16)

    tail = ext_ref[tm:tm + POOL_HALO, :]
    ext_ref[0:POOL_HALO, :] = tail

    @pl.when(tib == tiles_per_batch - 1)
    def _():
        tail_ref[...] = tail

    y = _dot(cat_ref[...], wout_ref[...])
    x1_ref[...] = x + _rms(y, gpost_ref[...])


def _mixer_ab(x, prev, gpre, gpost, w_in, ln_g, ln_b, w_s, b_s, w_map, p_scale, w_out,
              *, batch, tm, chunk, start_pos, emit_vn):
    t, d = x.shape
    aw = 4 * LANES
    tiles_per_batch = t // batch // tm
    kern = functools.partial(_mixer_ab_kernel, tm=tm, chunk=chunk, tiles_per_batch=tiles_per_batch,
                             start_pos=start_pos, emit_vn=emit_vn)
    out_shape = [jax.ShapeDtypeStruct((t, d), F32), jax.ShapeDtypeStruct((batch, POOL_HALO, aw), F32)]
    out_specs = [pl.BlockSpec((tm, d), lambda i: (i, 0)),
                 pl.BlockSpec((None, POOL_HALO, aw), lambda i: (i // tiles_per_batch, 0, 0))]
    if emit_vn:
        out_shape.append(jax.ShapeDtypeStruct((t, aw), F32))
        out_specs.append(pl.BlockSpec((tm, aw), lambda i: (i, 0)))
    return pl.pallas_call(
        kern,
        out_shape=out_shape,
        grid=(t // tm,),
        in_specs=[
            pl.BlockSpec((tm, d), lambda i: (i, 0)),
            pl.BlockSpec((None, POOL_HALO, aw), lambda i: (i // tiles_per_batch, 0, 0)),
            _full((1, d)), _full((1, d)), _full(w_in.shape), _full((1, aw)), _full((1, aw)),
            _full(w_s.shape), _full(b_s.shape), _full(w_map.shape), _full((1, aw)), _full(w_out.shape),
        ],
        out_specs=out_specs,
        scratch_shapes=[pltpu.VMEM((tm + POOL_HALO, aw), F32), pltpu.VMEM((tm, 2 * aw), BF16)],
        compiler_params=_params("arbitrary"),
        name="mixer_ab",
    )(x, prev, gpre, gpost, w_in, ln_g, ln_b, w_s, b_s, w_map, p_scale, w_out)


def _ffn_kernel(*refs, ff_chunk, has_proj):
    if has_proj:
        x_ref, o_ref, wo_ref, gmix_ref, gpre_ref, gpost_ref, wg_ref, wu_ref, wd_ref, out_ref = refs
        x = x_ref[...] + _rms(_dot(o_ref[...], wo_ref[...]), gmix_ref[...])
    else:
        x_ref, gpre_ref, gpost_ref, wg_ref, wu_ref, wd_ref, out_ref = refs
        x = x_ref[...]
    h = _rms(x, gpre_ref[...]).astype(BF16)
    d_ff = wg_ref.shape[1]
    f = None
    for c0 in range(0, d_ff, ff_chunk):
        c1 = min(c0 + ff_chunk, d_ff)
        act = jax.nn.silu(_dot(h, wg_ref[:, c0:c1])) * _dot(h, wu_ref[:, c0:c1])
        part = _dot(act.astype(BF16), wd_ref[c0:c1, :])
        f = part if f is None else f + part
    out_ref[...] = x + _rms(f, gpost_ref[...])


def _ffn(x, gpre, gpost, wg, wu, wd, *, tm, proj=None):
    t, d = x.shape
    row = pl.BlockSpec((tm, d), lambda i: (i, 0))
    args, specs = [x], [row]
    if proj is not None:
        o, wo, gmix = proj
        args += [o, wo, gmix]
        specs += [row, _full(wo.shape), _full((1, d))]
    args += [gpre, gpost, wg, wu, wd]
    specs += [_full((1, d)), _full((1, d)), _full(wg.shape), _full(wu.shape), _full(wd.shape)]
    return pl.pallas_call(
        functools.partial(_ffn_kernel, ff_chunk=512, has_proj=proj is not None),
        out_shape=jax.ShapeDtypeStruct((t, d), F32),
        grid=(t // tm,),
        in_specs=specs,
        out_specs=row,
        compiler_params=_params("parallel"),
        name="ffn_proj" if proj is not None else "ffn",
    )(*args)


Q_SCALE = HEAD_DIM ** -0.5 * 1.4426950408889634


def _qkv_rows_kernel(x_ref, gpre_ref, w_ref, k_ref, v_ref, qkv_ref):
    d = x_ref.shape[1]
    h = _rms(x_ref[...], gpre_ref[...]).astype(BF16)
    qkv = _dot(h, w_ref[...])
    k_ref[...] = qkv[:, d:2 * d]
    v_ref[...] = qkv[:, 2 * d:]
    qkv_ref[:, :d] = (qkv[:, :d] * Q_SCALE).astype(BF16)
    qkv_ref[:, d:] = qkv[:, d:].astype(BF16)


def _qkv_rows(x, gpre, w_qkv, *, tm):
    t, d = x.shape
    row = pl.BlockSpec((tm, d), lambda i: (i, 0))
    return pl.pallas_call(
        _qkv_rows_kernel,
        out_shape=[jax.ShapeDtypeStruct((t, d), F32), jax.ShapeDtypeStruct((t, d), F32),
                   jax.ShapeDtypeStruct((t, 3 * d), BF16)],
        grid=(t // tm,),
        in_specs=[row, _full((1, d)), _full(w_qkv.shape)],
        out_specs=[row, row, pl.BlockSpec((tm, 3 * d), lambda i: (i, 0))],
        compiler_params=_params("parallel"),
        name="qkv_rows",
    )(x, gpre, w_qkv)


def _qkv_cols_kernel(x_ref, gpre_ref, wq_ref, wkt_ref, wvt_ref, q_ref, kt_ref, vt_ref, ktb_ref, vtb_ref):
    h = _rms(x_ref[...], gpre_ref[...]).astype(BF16)
    q_ref[...] = (_dot(h, wq_ref[...]) * Q_SCALE).astype(BF16)
    nt = (((1,), (1,)), ((), ()))
    kt = lax.dot_general(wkt_ref[...], h, nt, preferred_element_type=F32)
    kt_ref[...] = kt
    ktb_ref[...] = kt.astype(BF16)
    vt = lax.dot_general(wvt_ref[...], h, nt, preferred_element_type=F32)
    vt_ref[...] = vt
    vtb_ref[...] = vt.astype(BF16)


def _qkv_cols(x, gpre, wq, wkt, wvt, *, batch, tm):
    t, d = x.shape
    seq = t // batch
    per = seq // tm
    row = pl.BlockSpec((tm, d), lambda i: (i, 0))
    col = pl.BlockSpec((None, d, tm), lambda i: (i // per, 0, i % per))
    return pl.pallas_call(
        _qkv_cols_kernel,
        out_shape=[jax.ShapeDtypeStruct((t, d), BF16),
                   jax.ShapeDtypeStruct((batch, d, seq), F32), jax.ShapeDtypeStruct((batch, d, seq), F32),
                   jax.ShapeDtypeStruct((batch, d, seq), BF16), jax.ShapeDtypeStruct((batch, d, seq), BF16)],
        grid=(t // tm,),
        in_specs=[row, _full((1, d)), _full(wq.shape), _full(wkt.shape), _full(wvt.shape)],
        out_specs=[row, col, col, col, col],
        compiler_params=_params("parallel"),
        name="qkv_cols",
    )(x, gpre, wq, wkt, wvt)


ATTN_PAIRS = 2
SIGN_BIT = -2 ** 31


def _suffix_matrix(tk):
    j = lax.broadcasted_iota(jnp.int32, (tk, tk), 0)
    s = lax.broadcasted_iota(jnp.int32, (tk, tk), 1)
    return jnp.where(j > s, 1.0, 0.0).astype(BF16)


def _softplus2(z):
    neg_abs = pltpu.bitcast(pltpu.bitcast(z, jnp.int32) | SIGN_BIT, F32)
    return jnp.maximum(z, 0.0) + jnp.log(1.0 + jnp.exp2(neg_abs)) * 1.4426950408889634


def _sb_run(qs, ks, vs, later_ref, acc_ref, suffix, diagonal, transposed, stack):
    nt = (((1,), (1,)), ((), ()))
    n = len(qs)
    sub = suffix.shape[0]
    if transposed:
        zs = [_dot(q, k) for q, k in zip(qs, ks)]
    else:
        zs = [lax.dot_general(q, k, nt, preferred_element_type=F32) for q, k in zip(qs, ks)]
    tq, keys = zs[0].shape
    m = keys // sub
    sps = [_softplus2(z) for z in zs]
    if diagonal:
        assert m == 1
        causal = lax.broadcasted_iota(jnp.int32, (tq, sub), 1) < lax.broadcasted_iota(jnp.int32, (tq, sub), 0)
        sps = [jnp.where(causal, sp, 0.0) for sp in sps]
    spb = [sp.astype(BF16) for sp in sps]
    if stack:
        order = [(j, c) for j in range(m) for c in range(n)]
        cs_all = _dot(jnp.concatenate([spb[c][:, j * sub:(j + 1) * sub] for j, c in order], axis=0), suffix)
        css = [[cs_all[(j * n + c) * tq:(j * n + c + 1) * tq] for c in range(n)] for j in range(m)]
    else:
        css = [[_dot(spb[c][:, j * sub:(j + 1) * sub], suffix) for c in range(n)] for j in range(m)]
    aa = []
    for c in range(n):
        later = later_ref[c]
        offs = [None] * m
        for j in reversed(range(m)):
            offs[j] = css[j][c] + jnp.concatenate([later] * (sub // LANES), axis=1)
            total = css[j][c][:, 0:1] + sps[c][:, j * sub:j * sub + 1]
            later = later + jnp.broadcast_to(total, later.shape)
        later_ref[c] = later
        a = jnp.exp2(zs[c] - sps[c] - jnp.concatenate(offs, axis=1))
        if diagonal:
            a = jnp.where(causal, a, 0.0)
        aa.append(a.astype(BF16))
    if transposed:
        pv = [lax.dot_general(a, v, nt, preferred_element_type=F32) for a, v in zip(aa, vs)]
    else:
        pv = [_dot(a, v) for a, v in zip(aa, vs)]
    for c in range(n):
        acc_ref[c] += pv[c]


def _sb_heads(q, later_ref, acc_ref, stack=False):
    tq = q.shape[0]
    pairs = q.shape[1] // LANES
    suffix = _suffix_matrix(ATTN_BLOCK)
    first = lax.broadcasted_iota(jnp.int32, (tq, LANES), 1) < HEAD_DIM
    zero = jnp.zeros((tq, LANES), q.dtype)
    qs = []
    for p in range(pairs):
        q2 = q[:, p * LANES:(p + 1) * LANES]
        qs += [jnp.where(first, q2, zero), jnp.where(first, zero, q2)]
    later_ref[...] = jnp.zeros_like(later_ref)
    acc_ref[...] = jnp.zeros_like(acc_ref)

    def run(kv_of_pair, diagonal=False, transposed=True):
        kvs = [kv_of_pair(p) for p in range(pairs) for _ in range(2)]
        _sb_run(qs, [kv[0] for kv in kvs], [kv[1] for kv in kvs], later_ref, acc_ref, suffix, diagonal,
                transposed, stack)

    def result():
        return jnp.concatenate([jnp.where(first, acc_ref[2 * p], acc_ref[2 * p + 1]) for p in range(pairs)], axis=1)

    return run, result


def _attn_prompt_kernel(q_ref, kt_ref, vt_ref, o_ref, later_ref, acc_ref):
    blk = ATTN_BLOCK
    qi = pl.program_id(2)
    run, result = _sb_heads(q_ref[...], later_ref, acc_ref)

    def kv(kb, nblk):
        keys = pl.ds(pl.multiple_of(kb * blk, blk), nblk * blk)
        return lambda p: (kt_ref[p * LANES:(p + 1) * LANES, keys], vt_ref[p * LANES:(p + 1) * LANES, keys])

    run(kv(qi, 1), diagonal=True)

    @pl.when(qi % 2 == 1)
    def _():
        run(kv(qi - 1, 1))

    npairs = qi // 2

    @pl.loop(0, npairs)
    def _(i):
        run(kv(2 * (npairs - 1 - i), 2))

    o_ref[...] = result().astype(BF16)


def _attn_prompt(q, kt, vt):
    t, d = q.shape
    batch, _, seq = kt.shape
    blk, width = ATTN_BLOCK, ATTN_PAIRS * LANES
    nq = seq // blk
    kvblk = pl.BlockSpec((None, width, seq), lambda b, p, i: (b, p, 0))
    qblk = pl.BlockSpec((blk, width), lambda b, p, i: (b * nq + i, p))
    chains = pltpu.VMEM((2 * ATTN_PAIRS, blk, LANES), F32)
    return pl.pallas_call(
        _attn_prompt_kernel,
        out_shape=jax.ShapeDtypeStruct((t, d), BF16),
        grid=(batch, d // width, nq),
        in_specs=[qblk, kvblk, kvblk],
        out_specs=qblk,
        scratch_shapes=[chains, chains],
        compiler_params=_params("parallel", "parallel", "arbitrary"),
        name="attn_prompt",
    )(q, kt, vt)


def _attn_sample_kernel(q_ref, kn_ref, vn_ref, kct_ref, vct_ref, o_ref, later_ref, acc_ref):
    tq = q_ref.shape[0]
    run, result = _sb_heads(q_ref[...], later_ref, acc_ref, stack=True)
    pad = jnp.zeros((ATTN_BLOCK - tq, LANES), BF16)

    def own(p):
        cols = slice(p * LANES, (p + 1) * LANES)
        return (jnp.concatenate([kn_ref[:, cols], pad], axis=0), jnp.concatenate([vn_ref[:, cols], pad], axis=0))

    def cache(p):
        rows = slice(p * LANES, (p + 1) * LANES)
        return kct_ref[rows, :].astype(BF16), vct_ref[rows, :].astype(BF16)

    run(own, diagonal=True, transposed=False)
    run(cache)
    o_ref[...] = result().astype(BF16)


def _attn_sample(qkv, cache_kt, cache_vt, *, seq):
    t = qkv.shape[0]
    d = qkv.shape[1] // 3
    batch, _, past = cache_kt.shape
    width = ATTN_PAIRS * LANES
    ngrp = d // width
    cblk = pl.BlockSpec((None, width, past), lambda b, p: (b, p, 0))
    chains = pltpu.VMEM((2 * ATTN_PAIRS, seq, LANES), F32)
    return pl.pallas_call(
        _attn_sample_kernel,
        out_shape=jax.ShapeDtypeStruct((t, d), BF16),
        grid=(batch, ngrp),
        in_specs=[
            pl.BlockSpec((seq, width), lambda b, p: (b, p)),
            pl.BlockSpec((seq, width), lambda b, p: (b, ngrp + p)),
            pl.BlockSpec((seq, width), lambda b, p: (b, 2 * ngrp + p)),
            cblk, cblk,
        ],
        out_specs=pl.BlockSpec((seq, width), lambda b, p: (b, p)),
        scratch_shapes=[chains, chains],
        compiler_params=_params("parallel", "parallel"),
        name="attn_sample",
    )(qkv, qkv, qkv, cache_kt, cache_vt)


def _features_first(kv):
    b, s, h, e = kv.shape
    return jnp.transpose(kv, (0, 2, 3, 1)).reshape(b, h * e, s)


def _tokens_first(kvt, heads):
    b, d, s = kvt.shape
    return jnp.transpose(kvt.reshape(b, heads, d // heads, s), (0, 3, 1, 2))[None]


def _trunk(x, prev_pool, cache_kv, w, *, is_prompt, tm_mix, tm_tok):
    batch, seq, d = x.shape
    heads = d // HEAD_DIM
    t = batch * seq
    x = x.reshape(t, d)
    prev = jnp.pad(prev_pool, ((0, 0), (POOL_HALO - prev_pool.shape[1], 0), (0, 0)))
    chunk = 2 * CHUNK if is_prompt else seq
    outs = _mixer_ab(x, prev, w["norm_mix_pre"][0], w["norm_mix_post"][0], w["w_in_ab"], w["ln_v_g"],
                     w["ln_v_b"], w["w_spatial"], w["b_spatial"][:, :chunk].T, w["w_pool_map"],
                     w["pool_scale"], w["w_out_ab"], batch=batch, tm=tm_mix, chunk=chunk,
                     start_pos=0 if is_prompt else cache_kv[0].shape[1], emit_vn=not is_prompt)
    x, tail = outs[0], outs[1]
    x = _ffn(x, w["norm_ffn_pre"][0], w["norm_ffn_post"][0], w["w_gate"][0], w["w_up"][0], w["w_down"][0],
             tm=tm_tok)
    if is_prompt:
        q, kt, vt, ktb, vtb = _qkv_cols(x, w["norm_mix_pre"][1], w["w_q"], w["w_kt"], w["w_vt"],
                                        batch=batch, tm=tm_tok)
        o = _attn_prompt(q, ktb, vtb)
        k, v = _tokens_first(kt, heads), _tokens_first(vt, heads)
    else:
        k, v, qkv = _qkv_rows(x, w["norm_mix_pre"][1], w["w_qkv"], tm=tm_tok)
        o = _attn_sample(qkv, _features_first(cache_kv[0]), _features_first(cache_kv[1]), seq=seq)
        k, v = k.reshape(1, batch, seq, heads, HEAD_DIM), v.reshape(1, batch, seq, heads, HEAD_DIM)
    x = _ffn(x, w["norm_ffn_pre"][1], w["norm_ffn_post"][1], w["w_gate"][1], w["w_up"][1], w["w_down"][1],
             tm=tm_tok, proj=(o, w["w_o_sb"], w["norm_mix_post"][1]))
    res = [x.reshape(batch, seq, d), tail[None, :, 1:, :], k, v]
    if not is_prompt:
        res.append(outs[2].reshape(1, batch, seq, -1))
    return res


def kernel(x_prompt, x_sample, state_pool, cache_k, cache_v, norm_mix_pre, norm_mix_post, norm_ffn_pre,
           norm_ffn_post, w_in_ab, ln_v_g, ln_v_b, w_spatial, b_spatial, w_pool_map, pool_scale, w_out_ab,
           w_qkv, w_o_sb, w_gate, w_up, w_down):
    depth, d = norm_mix_pre.shape
    assert depth == 2 and w_in_ab.shape[0] == 1 and w_qkv.shape[0] == 1
    wqkv = w_qkv[0].astype(BF16)
    w = dict(
        norm_mix_pre=norm_mix_pre.reshape(depth, 1, d), norm_mix_post=norm_mix_post.reshape(depth, 1, d),
        norm_ffn_pre=norm_ffn_pre.reshape(depth, 1, d), norm_ffn_post=norm_ffn_post.reshape(depth, 1, d),
        w_in_ab=w_in_ab[0].astype(BF16), ln_v_g=ln_v_g, ln_v_b=ln_v_b, w_spatial=w_spatial[0],
        b_spatial=b_spatial[0], w_pool_map=w_pool_map[0].astype(BF16), pool_scale=pool_scale,
        w_out_ab=w_out_ab[0].astype(BF16), w_qkv=wqkv, w_q=wqkv[:, :d], w_kt=wqkv[:, d:2 * d].T,
        w_vt=wqkv[:, 2 * d:].T, w_o_sb=w_o_sb[0].astype(BF16),
        w_gate=w_gate.astype(BF16), w_up=w_up.astype(BF16), w_down=w_down.astype(BF16),
    )
    batch = x_prompt.shape[0]
    zero_pool = jnp.zeros((batch,) + state_pool.shape[2:], F32)
    y_p, pool_p, k_p, v_p = _trunk(x_prompt, zero_pool, None, w, is_prompt=True, tm_mix=256, tm_tok=512)
    y_s, pool_s, k_s, v_s, vn_s = _trunk(x_sample, state_pool[0], (cache_k[0], cache_v[0]), w,
                                         is_prompt=False, tm_mix=x_sample.shape[1],
                                         tm_tok=x_sample.shape[0] * x_sample.shape[1])
    return (y_p, y_s, pool_p, k_p, v_p, pool_s, k_s, v_s, vn_s)
```

```python
import functools

import jax
import jax.numpy as jnp
from jax import lax
from jax.experimental import pallas as pl
from jax.experimental.pallas import tpu as pltpu

F32 = jnp.float32
BF16 = jnp.bfloat16

EPS = 1e-6
CHUNK = 64
POOL_WINDOWS = (2, 4, 8, 16)
POOL_HALO = 16
LANES = 128
HEAD_DIM = 64
ATTN_BLOCK = 256
VMEM_LIMIT = 56 * 1024 * 1024


def _rms(x, g):
    return x * lax.rsqrt(jnp.mean(x * x, axis=-1, keepdims=True) + EPS) * g


def _dot(a, b):
    return jnp.dot(a, b, preferred_element_type=F32)


def _params(*sem):
    return pltpu.CompilerParams(dimension_semantics=sem, vmem_limit_bytes=VMEM_LIMIT)


def _full(shape):
    return pl.BlockSpec(shape, lambda *_: (0,) * len(shape))


def _mixer_ab_kernel(x_ref, prev_ref, gpre_ref, gpost_ref, win_ref, lng_ref, lnb_ref, ws_ref,
                     bs_ref, wmap_ref, pscale_ref, wout_ref, *rest,
                     tm, chunk, tiles_per_batch, start_pos, emit_vn):
    if emit_vn:
        x1_ref, tail_ref, vn_ref, ext_ref, cat_ref = rest
    else:
        x1_ref, tail_ref, ext_ref, cat_ref = rest
    aw = 4 * LANES
    tib = pl.program_id(0) % tiles_per_batch

    @pl.when(tib == 0)
    def _():
        ext_ref[0:POOL_HALO, :] = prev_ref[...]

    x = x_ref[...]
    h = _rms(x, gpre_ref[...]).astype(BF16)
    z = _dot(h, win_ref[...])
    zuv = z[:, :2 * aw]
    uv = 0.5 * zuv * (1.0 + lax.erf(zuv * (0.5 ** 0.5)))
    u, v = uv[:, :aw], uv[:, aw:]
    mu = jnp.mean(v, axis=-1, keepdims=True)
    vc = v - mu
    vn = vc * lax.rsqrt(jnp.mean(vc * vc, axis=-1, keepdims=True) + EPS) * lng_ref[...] + lnb_ref[...]
    if emit_vn:
        vn_ref[...] = vn

    ri = lax.broadcasted_iota(jnp.int32, (chunk, LANES), 0)
    ci = lax.broadcasted_iota(jnp.int32, (chunk, LANES), 1)
    allowed = (ci // CHUNK <= ri // CHUNK) & (ci < chunk)
    bias = bs_ref[...]
    vnb = vn.astype(BF16)
    for g in range(4):
        wg = jnp.where(allowed, ws_ref[g, 0:chunk, :], 0.0).astype(BF16)
        bg = bias[:, g:g + 1]
        for c in range(tm // chunk):
            vblk = vnb[c * chunk:(c + 1) * chunk, g * LANES:(g + 1) * LANES]
            if chunk < LANES:
                vblk = jnp.concatenate([vblk, jnp.zeros((LANES - chunk, LANES), BF16)], axis=0)
            s = _dot(wg, vblk) + bg
            ublk = u[c * chunk:(c + 1) * chunk, g * LANES:(g + 1) * LANES]
            cat_ref[c * chunk:(c + 1) * chunk, g * LANES:(g + 1) * LANES] = (ublk * s).astype(BF16)

    p = z[:, 2 * aw:]
    ext_ref[POOL_HALO:POOL_HALO + tm, :] = p
    e1 = ext_ref[...]
    e2 = e1 + pltpu.roll(e1, 1, axis=0)
    e4 = e2[:, LANES:] + pltpu.roll(e2[:, LANES:], 2, axis=0)
    e8 = e4[:, LANES:] + pltpu.roll(e4[:, LANES:], 4, axis=0)
    e16 = e8[:, LANES:] + pltpu.roll(e8[:, LANES:], 8, axis=0)
    wins = (e2[POOL_HALO:, :LANES], e4[POOL_HALO:, :LANES], e8[POOL_HALO:, :LANES], e16[POOL_HALO:, :])
    pos = start_pos + tib * tm + lax.broadcasted_iota(jnp.int32, (tm, 1), 0)
    pscale = pscale_ref[...]
    for g, w in enumerate(POOL_WINDOWS):
        inv_cnt = 1.0 / jnp.minimum(w, pos + 1).astype(F32)
        d = wins[g] * inv_cnt - p[:, g * LANES:(g + 1) * LANES]
        yb = _dot(d.astype(BF16), wmap_ref[g]) * pscale[:, g * LANES:(g + 1) * LANES]
        cat_ref[:, aw + g * LANES:aw + (g + 1) * LANES] = yb.astype(BF16)

    tail = ext_ref[tm:tm + POOL_HALO, :]
    ext_ref[0:POOL_HALO, :] = tail

    @pl.when(tib == tiles_per_batch - 1)
    def _():
        tail_ref[...] = tail

    y = _dot(cat_ref[...], wout_ref[...])
    x1_ref[...] = x + _rms(y, gpost_ref[...])


def _mixer_ab(x, prev, gpre, gpost, w_in, ln_g, ln_b, w_s, b_s, w_map, p_scale, w_out,
              *, batch, tm, chunk, start_pos, emit_vn):
    t, d = x.shape
    aw = 4 * LANES
    tiles_per_batch = t // batch // tm
    kern = functools.partial(_mixer_ab_kernel, tm=tm, chunk=chunk, tiles_per_batch=tiles_per_batch,
                             start_pos=start_pos, emit_vn=emit_vn)
    out_shape = [jax.ShapeDtypeStruct((t, d), F32), jax.ShapeDtypeStruct((batch, POOL_HALO, aw), F32)]
    out_specs = [pl.BlockSpec((tm, d), lambda i: (i, 0)),
                 pl.BlockSpec((None, POOL_HALO, aw), lambda i: (i // tiles_per_batch, 0, 0))]
    if emit_vn:
        out_shape.append(jax.ShapeDtypeStruct((t, aw), F32))
        out_specs.append(pl.BlockSpec((tm, aw), lambda i: (i, 0)))
    return pl.pallas_call(
        kern,
        out_shape=out_shape,
        grid=(t // tm,),
        in_specs=[
            pl.BlockSpec((tm, d), lambda i: (i, 0)),
            pl.BlockSpec((None, POOL_HALO, aw), lambda i: (i // tiles_per_batch, 0, 0)),
            _full((1, d)), _full((1, d)), _full(w_in.shape), _full((1, aw)), _full((1, aw)),
            _full(w_s.shape), _full(b_s.shape), _full(w_map.shape), _full((1, aw)), _full(w_out.shape),
        ],
        out_specs=out_specs,
        scratch_shapes=[pltpu.VMEM((tm + POOL_HALO, aw), F32), pltpu.VMEM((tm, 2 * aw), BF16)],
        compiler_params=_params("arbitrary"),
        name="mixer_ab",
    )(x, prev, gpre, gpost, w_in, ln_g, ln_b, w_s, b_s, w_map, p_scale, w_out)


def _ffn_kernel(*refs, ff_chunk, has_proj):
    if has_proj:
        x_ref, o_ref, wo_ref, gmix_ref, gpre_ref, gpost_ref, wg_ref, wu_ref, wd_ref, out_ref = refs
        x = x_ref[...] + _rms(_dot(o_ref[...], wo_ref[...]), gmix_ref[...])
    else:
        x_ref, gpre_ref, gpost_ref, wg_ref, wu_ref, wd_ref, out_ref = refs
        x = x_ref[...]
    h = _rms(x, gpre_ref[...]).astype(BF16)
    d_ff = wg_ref.shape[1]
    f = None
    for c0 in range(0, d_ff, ff_chunk):
        c1 = min(c0 + ff_chunk, d_ff)
        act = jax.nn.silu(_dot(h, wg_ref[:, c0:c1])) * _dot(h, wu_ref[:, c0:c1])
        part = _dot(act.astype(BF16), wd_ref[c0:c1, :])
        f = part if f is None else f + part
    out_ref[...] = x + _rms(f, gpost_ref[...])


def _ffn(x, gpre, gpost, wg, wu, wd, *, tm, proj=None):
    t, d = x.shape
    row = pl.BlockSpec((tm, d), lambda i: (i, 0))
    args, specs = [x], [row]
    if proj is not None:
        o, wo, gmix = proj
        args += [o, wo, gmix]
        specs += [row, _full(wo.shape), _full((1, d))]
    args += [gpre, gpost, wg, wu, wd]
    specs += [_full((1, d)), _full((1, d)), _full(wg.shape), _full(wu.shape), _full(wd.shape)]
    return pl.pallas_call(
        functools.partial(_ffn_kernel, ff_chunk=512, has_proj=proj is not None),
        out_shape=jax.ShapeDtypeStruct((t, d), F32),
        grid=(t // tm,),
        in_specs=specs,
        out_specs=row,
        compiler_params=_params("parallel"),
        name="ffn_proj" if proj is not None else "ffn",
    )(*args)


Q_SCALE = HEAD_DIM ** -0.5 * 1.4426950408889634


def _qkv_rows_kernel(x_ref, gpre_ref, w_ref, k_ref, v_ref, qkv_ref):
    d = x_ref.shape[1]
    h = _rms(x_ref[...], gpre_ref[...]).astype(BF16)
    qkv = _dot(h, w_ref[...])
    k_ref[...] = qkv[:, d:2 * d]
    v_ref[...] = qkv[:, 2 * d:]
    qkv_ref[:, :d] = (qkv[:, :d] * Q_SCALE).astype(BF16)
    qkv_ref[:, d:] = qkv[:, d:].astype(BF16)


def _qkv_rows(x, gpre, w_qkv, *, tm):
    t, d = x.shape
    row = pl.BlockSpec((tm, d), lambda i: (i, 0))
    return pl.pallas_call(
        _qkv_rows_kernel,
        out_shape=[jax.ShapeDtypeStruct((t, d), F32), jax.ShapeDtypeStruct((t, d), F32),
                   jax.ShapeDtypeStruct((t, 3 * d), BF16)],
        grid=(t // tm,),
        in_specs=[row, _full((1, d)), _full(w_qkv.shape)],
        out_specs=[row, row, pl.BlockSpec((tm, 3 * d), lambda i: (i, 0))],
        compiler_params=_params("parallel"),
        name="qkv_rows",
    )(x, gpre, w_qkv)


def _qkv_cols_kernel(x_ref, gpre_ref, wq_ref, wkt_ref, wvt_ref, q_ref, kt_ref, vt_ref, ktb_ref, vtb_ref):
    h = _rms(x_ref[...], gpre_ref[...]).astype(BF16)
    q_ref[...] = (_dot(h, wq_ref[...]) * Q_SCALE).astype(BF16)
    nt = (((1,), (1,)), ((), ()))
    kt = lax.dot_general(wkt_ref[...], h, nt, preferred_element_type=F32)
    kt_ref[...] = kt
    ktb_ref[...] = kt.astype(BF16)
    vt = lax.dot_general(wvt_ref[...], h, nt, preferred_element_type=F32)
    vt_ref[...] = vt
    vtb_ref[...] = vt.astype(BF16)


def _qkv_cols(x, gpre, wq, wkt, wvt, *, batch, tm):
    t, d = x.shape
    seq = t // batch
    per = seq // tm
    row = pl.BlockSpec((tm, d), lambda i: (i, 0))
    col = pl.BlockSpec((None, d, tm), lambda i: (i // per, 0, i % per))
    return pl.pallas_call(
        _qkv_cols_kernel,
        out_shape=[jax.ShapeDtypeStruct((t, d), BF16),
                   jax.ShapeDtypeStruct((batch, d, seq), F32), jax.ShapeDtypeStruct((batch, d, seq), F32),
                   jax.ShapeDtypeStruct((batch, d, seq), BF16), jax.ShapeDtypeStruct((batch, d, seq), BF16)],
        grid=(t // tm,),
        in_specs=[row, _full((1, d)), _full(wq.shape), _full(wkt.shape), _full(wvt.shape)],
        out_specs=[row, col, col, col, col],
        compiler_params=_params("parallel"),
        name="qkv_cols",
    )(x, gpre, wq, wkt, wvt)


ATTN_PAIRS = 2
LATER_DONE = 160.0
SOFTPLUS_LINEAR = 64.0


def _suffix_matrix(tk):
    j = lax.broadcasted_iota(jnp.int32, (tk, tk), 0)
    s = lax.broadcasted_iota(jnp.int32, (tk, tk), 1)
    return jnp.where(j > s, 1.0, 0.0).astype(BF16)


def _softplus2(z):
    return jnp.where(z > SOFTPLUS_LINEAR, z, jnp.log(1.0 + jnp.exp2(z)) * 1.4426950408889634)


def _sb_run(qs, ks, vs, later_ref, acc_ref, suffix, diagonal, transposed, stack):
    nt = (((1,), (1,)), ((), ()))
    n = len(qs)
    sub = suffix.shape[0]
    if transposed:
        zs = [_dot(q, k) for q, k in zip(qs, ks)]
    else:
        zs = [lax.dot_general(q, k, nt, preferred_element_type=F32) for q, k in zip(qs, ks)]
    tq, keys = zs[0].shape
    m = keys // sub
    sps = [_softplus2(z) for z in zs]
    if diagonal:
        col = lax.broadcasted_iota(jnp.int32, (tq, keys), 1)
        causal = col < lax.broadcasted_iota(jnp.int32, (tq, keys), 0) + (m - 1) * sub
        sps = [jnp.where(causal, sp, 0.0) for sp in sps]
    spb = [sp.astype(BF16) for sp in sps]
    if stack:
        order = [(j, c) for j in range(m) for c in range(n)]
        cs_all = _dot(jnp.concatenate([spb[c][:, j * sub:(j + 1) * sub] for j, c in order], axis=0), suffix)
        css = [[cs_all[(j * n + c) * tq:(j * n + c + 1) * tq] for c in range(n)] for j in range(m)]
    else:
        css = [[_dot(spb[c][:, j * sub:(j + 1) * sub], suffix) for c in range(n)] for j in range(m)]
    aa = []
    for c in range(n):
        later = later_ref[c]
        offs = [None] * m
        for j in reversed(range(m)):
            offs[j] = css[j][c] + jnp.concatenate([later] * (sub // LANES), axis=1)
            total = css[j][c][:, 0:1] + sps[c][:, j * sub:j * sub + 1]
            later = later + jnp.broadcast_to(total, later.shape)
        later_ref[c] = later
        a = jnp.exp2(zs[c] - sps[c] - jnp.concatenate(offs, axis=1))
        if diagonal:
            a = jnp.where(causal, a, 0.0)
        aa.append(a.astype(BF16))
    if transposed:
        pv = [lax.dot_general(a, v, nt, preferred_element_type=F32) for a, v in zip(aa, vs)]
    else:
        pv = [_dot(a, v) for a, v in zip(aa, vs)]
    for c in range(n):
        acc_ref[c] += pv[c]


def _sb_heads(q, later_ref, acc_ref, stack=False):
    tq = q.shape[0]
    pairs = q.shape[1] // LANES
    suffix = _suffix_matrix(ATTN_BLOCK)
    first = lax.broadcasted_iota(jnp.int32, (tq, LANES), 1) < HEAD_DIM
    zero = jnp.zeros((tq, LANES), q.dtype)
    qs = []
    for p in range(pairs):
        q2 = q[:, p * LANES:(p + 1) * LANES]
        qs += [jnp.where(first, q2, zero), jnp.where(first, zero, q2)]
    later_ref[...] = jnp.zeros_like(later_ref)
    acc_ref[...] = jnp.zeros_like(acc_ref)

    def run(kv_of_pair, diagonal=False, transposed=True):
        kvs = [kv_of_pair(p) for p in range(pairs) for _ in range(2)]
        _sb_run(qs, [kv[0] for kv in kvs], [kv[1] for kv in kvs], later_ref, acc_ref, suffix, diagonal,
                transposed, stack)

    def result():
        return jnp.concatenate([jnp.where(first, acc_ref[2 * p], acc_ref[2 * p + 1]) for p in range(pairs)], axis=1)

    return run, result


def _attn_prompt_kernel(q_ref, kt_ref, vt_ref, o_ref, later_ref, acc_ref):
    blk = ATTN_BLOCK
    qi = pl.program_id(2)
    run, result = _sb_heads(q_ref[...], later_ref, acc_ref)

    def kv(kb, nblk):
        keys = pl.ds(pl.multiple_of(kb * blk, blk), nblk * blk)
        return lambda p: (kt_ref[p * LANES:(p + 1) * LANES, keys], vt_ref[p * LANES:(p + 1) * LANES, keys])

    def unfinished():
        return jnp.min(later_ref[...]) < LATER_DONE

    @pl.when(qi == 0)
    def _():
        run(kv(0, 1), diagonal=True)

    @pl.when(qi > 0)
    def _():
        run(kv(qi - 1, 2), diagonal=True)

        def older(carry):
            kb, _ = carry
            run(kv(kb, 1))
            return kb - 1, unfinished()

        lax.while_loop(lambda c: (c[0] >= 0) & c[1], older, (qi - 2, unfinished()))

    o_ref[...] = result().astype(BF16)


def _attn_prompt(q, kt, vt):
    t, d = q.shape
    batch, _, seq = kt.shape
    blk, width = ATTN_BLOCK, ATTN_PAIRS * LANES
    nq = seq // blk
    kvblk = pl.BlockSpec((None, width, seq), lambda b, p, i: (b, p, 0))
    qblk = pl.BlockSpec((blk, width), lambda b, p, i: (b * nq + i, p))
    chains = pltpu.VMEM((2 * ATTN_PAIRS, blk, LANES), F32)
    return pl.pallas_call(
        _attn_prompt_kernel,
        out_shape=jax.ShapeDtypeStruct((t, d), BF16),
        grid=(batch, d // width, nq),
        in_specs=[qblk, kvblk, kvblk],
        out_specs=qblk,
        scratch_shapes=[chains, chains],
        compiler_params=_params("parallel", "parallel", "arbitrary"),
        name="attn_prompt",
    )(q, kt, vt)


def _attn_sample_kernel(q_ref, kn_ref, vn_ref, kct_ref, vct_ref, o_ref, later_ref, acc_ref):
    tq = q_ref.shape[0]
    run, result = _sb_heads(q_ref[...], later_ref, acc_ref, stack=True)
    pad = jnp.zeros((ATTN_BLOCK - tq, LANES), BF16)

    def own(p):
        cols = slice(p * LANES, (p + 1) * LANES)
        return (jnp.concatenate([kn_ref[:, cols], pad], axis=0), jnp.concatenate([vn_ref[:, cols], pad], axis=0))

    def cache(p):
        rows = slice(p * LANES, (p + 1) * LANES)
        return kct_ref[rows, :].astype(BF16), vct_ref[rows, :].astype(BF16)

    run(own, diagonal=True, transposed=False)
    run(cache)
    o_ref[...] = result().astype(BF16)


def _attn_sample(qkv, cache_kt, cache_vt, *, seq):
    t = qkv.shape[0]
    d = qkv.shape[1] // 3
    batch, _, past = cache_kt.shape
    width = ATTN_PAIRS * LANES
    ngrp = d // width
    cblk = pl.BlockSpec((None, width, past), lambda b, p: (b, p, 0))
    chains = pltpu.VMEM((2 * ATTN_PAIRS, seq, LANES), F32)
    return pl.pallas_call(
        _attn_sample_kernel,
        out_shape=jax.ShapeDtypeStruct((t, d), BF16),
        grid=(batch, ngrp),
        in_specs=[
            pl.BlockSpec((seq, width), lambda b, p: (b, p)),
            pl.BlockSpec((seq, width), lambda b, p: (b, ngrp + p)),
            pl.BlockSpec((seq, width), lambda b, p: (b, 2 * ngrp + p)),
            cblk, cblk,
        ],
        out_specs=pl.BlockSpec((seq, width), lambda b, p: (b, p)),
        scratch_shapes=[chains, chains],
        compiler_params=_params("parallel", "parallel"),
        name="attn_sample",
    )(qkv, qkv, qkv, cache_kt, cache_vt)


def _features_first(kv):
    b, s, h, e = kv.shape
    return jnp.transpose(kv, (0, 2, 3, 1)).reshape(b, h * e, s)


def _tokens_first(kvt, heads):
    b, d, s = kvt.shape
    return jnp.transpose(kvt.reshape(b, heads, d // heads, s), (0, 3, 1, 2))[None]


def _trunk(x, prev_pool, cache_kv, w, *, is_prompt, tm_mix, tm_tok):
    batch, seq, d = x.shape
    heads = d // HEAD_DIM
    t = batch * seq
    x = x.reshape(t, d)
    prev = jnp.pad(prev_pool, ((0, 0), (POOL_HALO - prev_pool.shape[1], 0), (0, 0)))
    chunk = 2 * CHUNK if is_prompt else seq
    outs = _mixer_ab(x, prev, w["norm_mix_pre"][0], w["norm_mix_post"][0], w["w_in_ab"], w["ln_v_g"],
                     w["ln_v_b"], w["w_spatial"], w["b_spatial"][:, :chunk].T, w["w_pool_map"],
                     w["pool_scale"], w["w_out_ab"], batch=batch, tm=tm_mix, chunk=chunk,
                     start_pos=0 if is_prompt else cache_kv[0].shape[1], emit_vn=not is_prompt)
    x, tail = outs[0], outs[1]
    x = _ffn(x, w["norm_ffn_pre"][0], w["norm_ffn_post"][0], w["w_gate"][0], w["w_up"][0], w["w_down"][0],
             tm=tm_tok)
    if is_prompt:
        q, kt, vt, ktb, vtb = _qkv_cols(x, w["norm_mix_pre"][1], w["w_q"], w["w_kt"], w["w_vt"],
                                        batch=batch, tm=tm_tok)
        o = _attn_prompt(q, ktb, vtb)
        k, v = _tokens_first(kt, heads), _tokens_first(vt, heads)
    else:
        k, v, qkv = _qkv_rows(x, w["norm_mix_pre"][1], w["w_qkv"], tm=tm_tok)
        o = _attn_sample(qkv, _features_first(cache_kv[0]), _features_first(cache_kv[1]), seq=seq)
        k, v = k.reshape(1, batch, seq, heads, HEAD_DIM), v.reshape(1, batch, seq, heads, HEAD_DIM)
    x = _ffn(x, w["norm_ffn_pre"][1], w["norm_ffn_post"][1], w["w_gate"][1], w["w_up"][1], w["w_down"][1],
             tm=tm_tok, proj=(o, w["w_o_sb"], w["norm_mix_post"][1]))
    res = [x.reshape(batch, seq, d), tail[None, :, 1:, :], k, v]
    if not is_prompt:
        res.append(outs[2].reshape(1, batch, seq, -1))
    return res


def kernel(x_prompt, x_sample, state_pool, cache_k, cache_v, norm_mix_pre, norm_mix_post, norm_ffn_pre,
           norm_ffn_post, w_in_ab, ln_v_g, ln_v_b, w_spatial, b_spatial, w_pool_map, pool_scale, w_out_ab,
           w_qkv, w_o_sb, w_gate, w_up, w_down):
    depth, d = norm_mix_pre.shape
    assert depth == 2 and w_in_ab.shape[0] == 1 and w_qkv.shape[0] == 1
    wqkv = w_qkv[0].astype(BF16)
    w = dict(
        norm_mix_pre=norm_mix_pre.reshape(depth, 1, d), norm_mix_post=norm_mix_post.reshape(depth, 1, d),
        norm_ffn_pre=norm_ffn_pre.reshape(depth, 1, d), norm_ffn_post=norm_ffn_post.reshape(depth, 1, d),
        w_in_ab=w_in_ab[0].astype(BF16), ln_v_g=ln_v_g, ln_v_b=ln_v_b, w_spatial=w_spatial[0],
        b_spatial=b_spatial[0], w_pool_map=w_pool_map[0].astype(BF16), pool_scale=pool_scale,
        w_out_ab=w_out_ab[0].astype(BF16), w_qkv=wqkv, w_q=wqkv[:, :d], w_kt=wqkv[:, d:2 * d].T,
        w_vt=wqkv[:, 2 * d:].T, w_o_sb=w_o_sb[0].astype(BF16),
        w_gate=w_gate.astype(BF16), w_up=w_up.astype(BF16), w_down=w_down.astype(BF16),
    )
    batch = x_prompt.shape[0]
    zero_pool = jnp.zeros((batch,) + state_pool.shape[2:], F32)
    y_p, pool_p, k_p, v_p = _trunk(x_prompt, zero_pool, None, w, is_prompt=True, tm_mix=256, tm_tok=512)
    y_s, pool_s, k_s, v_s, vn_s = _trunk(x_sample, state_pool[0], (cache_k[0], cache_v[0]), w,
                                         is_prompt=False, tm_mix=x_sample.shape[1],
                                         tm_tok=x_sample.shape[0] * x_sample.shape[1])
    return (y_p, y_s, pool_p, k_p, v_p, pool_s, k_s, v_s, vn_s)
```

```python
import functools

import jax
import jax.numpy as jnp
from jax import lax
from jax.experimental import pallas as pl
from jax.experimental.pallas import tpu as pltpu

F32 = jnp.float32
BF16 = jnp.bfloat16

EPS = 1e-6
CHUNK = 64
POOL_WINDOWS = (2, 4, 8, 16)
POOL_HALO = 16
LANES = 128
HEAD_DIM = 64
ATTN_BLOCK = 256
VMEM_LIMIT = 56 * 1024 * 1024


def _rms(x, g):
    return x * lax.rsqrt(jnp.mean(x * x, axis=-1, keepdims=True) + EPS) * g


def _dot(a, b):
    return jnp.dot(a, b, preferred_element_type=F32)


def _params(*sem):
    return pltpu.CompilerParams(dimension_semantics=sem, vmem_limit_bytes=VMEM_LIMIT)


def _full(shape):
    return pl.BlockSpec(shape, lambda *_: (0,) * len(shape))


def _mixer_ab_kernel(x_ref, prev_ref, gpre_ref, gpost_ref, win_ref, lng_ref, lnb_ref, ws_ref,
                     bs_ref, wmap_ref, pscale_ref, wout_ref, *rest,
                     tm, chunk, tiles_per_batch, start_pos, emit_vn):
    if emit_vn:
        x1_ref, tail_ref, vn_ref, ext_ref, cat_ref = rest
    else:
        x1_ref, tail_ref, ext_ref, cat_ref = rest
    aw = 4 * LANES
    tib = pl.program_id(0) % tiles_per_batch

    @pl.when(tib == 0)
    def _():
        ext_ref[0:POOL_HALO, :] = prev_ref[...]

    x = x_ref[...]
    h = _rms(x, gpre_ref[...]).astype(BF16)
    z = _dot(h, win_ref[...])
    zuv = z[:, :2 * aw]
    uv = 0.5 * zuv * (1.0 + lax.erf(zuv * (0.5 ** 0.5)))
    u, v = uv[:, :aw], uv[:, aw:]
    mu = jnp.mean(v, axis=-1, keepdims=True)
    vc = v - mu
    vn = vc * lax.rsqrt(jnp.mean(vc * vc, axis=-1, keepdims=True) + EPS) * lng_ref[...] + lnb_ref[...]
    if emit_vn:
        vn_ref[...] = vn

    ri = lax.broadcasted_iota(jnp.int32, (chunk, LANES), 0)
    ci = lax.broadcasted_iota(jnp.int32, (chunk, LANES), 1)
    allowed = (ci // CHUNK <= ri // CHUNK) & (ci < chunk)
    bias = bs_ref[...]
    vnb = vn.astype(BF16)
    for g in range(4):
        wg = jnp.where(allowed, ws_ref[g, 0:chunk, :], 0.0).astype(BF16)
        bg = bias[:, g:g + 1]
        for c in range(tm // chunk):
            vblk = vnb[c * chunk:(c + 1) * chunk, g * LANES:(g + 1) * LANES]
            if chunk < LANES:
                vblk = jnp.concatenate([vblk, jnp.zeros((LANES - chunk, LANES), BF16)], axis=0)
            s = _dot(wg, vblk) + bg
            ublk = u[c * chunk:(c + 1) * chunk, g * LANES:(g + 1) * LANES]
            cat_ref[c * chunk:(c + 1) * chunk, g * LANES:(g + 1) * LANES] = (ublk * s).astype(BF16)

    p = z[:, 2 * aw:]
    ext_ref[POOL_HALO:POOL_HALO + tm, :] = p
    e1 = ext_ref[...]
    e2 = e1 + pltpu.roll(e1, 1, axis=0)
    e4 = e2[:, LANES:] + pltpu.roll(e2[:, LANES:], 2, axis=0)
    e8 = e4[:, LANES:] + pltpu.roll(e4[:, LANES:], 4, axis=0)
    e16 = e8[:, LANES:] + pltpu.roll(e8[:, LANES:], 8, axis=0)
    wins = (e2[POOL_HALO:, :LANES], e4[POOL_HALO:, :LANES], e8[POOL_HALO:, :LANES], e16[POOL_HALO:, :])
    pos = start_pos + tib * tm + lax.broadcasted_iota(jnp.int32, (tm, 1), 0)
    pscale = pscale_ref[...]
    for g, w in enumerate(POOL_WINDOWS):
        inv_cnt = 1.0 / jnp.minimum(w, pos + 1).astype(F32)
        d = wins[g] * inv_cnt - p[:, g * LANES:(g + 1) * LANES]
        yb = _dot(d.astype(BF16), wmap_ref[g]) * pscale[:, g * LANES:(g + 1) * LANES]
        cat_ref[:, aw + g * LANES:aw + (g + 1) * LANES] = yb.astype(BF16)

    tail = ext_ref[tm:tm + POOL_HALO, :]
    ext_ref[0:POOL_HALO, :] = tail

    @pl.when(tib == tiles_per_batch - 1)
    def _():
        tail_ref[...] = tail

    y = _dot(cat_ref[...], wout_ref[...])
    x1_ref[...] = x + _rms(y, gpost_ref[...])


def _mixer_ab(x, prev, gpre, gpost, w_in, ln_g, ln_b, w_s, b_s, w_map, p_scale, w_out,
              *, batch, tm, chunk, start_pos, emit_vn):
    t, d = x.shape
    aw = 4 * LANES
    tiles_per_batch = t // batch // tm
    kern = functools.partial(_mixer_ab_kernel, tm=tm, chunk=chunk, tiles_per_batch=tiles_per_batch,
                             start_pos=start_pos, emit_vn=emit_vn)
    out_shape = [jax.ShapeDtypeStruct((t, d), F32), jax.ShapeDtypeStruct((batch, POOL_HALO, aw), F32)]
    out_specs = [pl.BlockSpec((tm, d), lambda i: (i, 0)),
                 pl.BlockSpec((None, POOL_HALO, aw), lambda i: (i // tiles_per_batch, 0, 0))]
    if emit_vn:
        out_shape.append(jax.ShapeDtypeStruct((t, aw), F32))
        out_specs.append(pl.BlockSpec((tm, aw), lambda i: (i, 0)))
    return pl.pallas_call(
        kern,
        out_shape=out_shape,
        grid=(t // tm,),
        in_specs=[
            pl.BlockSpec((tm, d), lambda i: (i, 0)),
            pl.BlockSpec((None, POOL_HALO, aw), lambda i: (i // tiles_per_batch, 0, 0)),
            _full((1, d)), _full((1, d)), _full(w_in.shape), _full((1, aw)), _full((1, aw)),
            _full(w_s.shape), _full(b_s.shape), _full(w_map.shape), _full((1, aw)), _full(w_out.shape),
        ],
        out_specs=out_specs,
        scratch_shapes=[pltpu.VMEM((tm + POOL_HALO, aw), F32), pltpu.VMEM((tm, 2 * aw), BF16)],
        compiler_params=_params("arbitrary"),
        name="mixer_ab",
    )(x, prev, gpre, gpost, w_in, ln_g, ln_b, w_s, b_s, w_map, p_scale, w_out)


def _ffn_kernel(*refs, ff_chunk, has_proj):
    if has_proj:
        x_ref, o_ref, wo_ref, gmix_ref, gpre_ref, gpost_ref, wg_ref, wu_ref, wd_ref, out_ref = refs
        x = x_ref[...] + _rms(_dot(o_ref[...], wo_ref[...]), gmix_ref[...])
    else:
        x_ref, gpre_ref, gpost_ref, wg_ref, wu_ref, wd_ref, out_ref = refs
        x = x_ref[...]
    h = _rms(x, gpre_ref[...]).astype(BF16)
    d_ff = wg_ref.shape[1]
    f = None
    for c0 in range(0, d_ff, ff_chunk):
        c1 = min(c0 + ff_chunk, d_ff)
        act = jax.nn.silu(_dot(h, wg_ref[:, c0:c1])) * _dot(h, wu_ref[:, c0:c1])
        part = _dot(act.astype(BF16), wd_ref[c0:c1, :])
        f = part if f is None else f + part
    out_ref[...] = x + _rms(f, gpost_ref[...])


def _ffn(x, gpre, gpost, wg, wu, wd, *, tm, proj=None):
    t, d = x.shape
    row = pl.BlockSpec((tm, d), lambda i: (i, 0))
    args, specs = [x], [row]
    if proj is not None:
        o, wo, gmix = proj
        args += [o, wo, gmix]
        specs += [row, _full(wo.shape), _full((1, d))]
    args += [gpre, gpost, wg, wu, wd]
    specs += [_full((1, d)), _full((1, d)), _full(wg.shape), _full(wu.shape), _full(wd.shape)]
    return pl.pallas_call(
        functools.partial(_ffn_kernel, ff_chunk=512, has_proj=proj is not None),
        out_shape=jax.ShapeDtypeStruct((t, d), F32),
        grid=(t // tm,),
        in_specs=specs,
        out_specs=row,
        compiler_params=_params("parallel"),
        name="ffn_proj" if proj is not None else "ffn",
    )(*args)


Q_SCALE = HEAD_DIM ** -0.5 * 1.4426950408889634


def _qkv_rows_kernel(x_ref, gpre_ref, w_ref, k_ref, v_ref, qkv_ref):
    d = x_ref.shape[1]
    h = _rms(x_ref[...], gpre_ref[...]).astype(BF16)
    qkv = _dot(h, w_ref[...])
    k_ref[...] = qkv[:, d:2 * d]
    v_ref[...] = qkv[:, 2 * d:]
    qkv_ref[:, :d] = (qkv[:, :d] * Q_SCALE).astype(BF16)
    qkv_ref[:, d:] = qkv[:, d:].astype(BF16)


def _qkv_rows(x, gpre, w_qkv, *, tm):
    t, d = x.shape
    row = pl.BlockSpec((tm, d), lambda i: (i, 0))
    return pl.pallas_call(
        _qkv_rows_kernel,
        out_shape=[jax.ShapeDtypeStruct((t, d), F32), jax.ShapeDtypeStruct((t, d), F32),
                   jax.ShapeDtypeStruct((t, 3 * d), BF16)],
        grid=(t // tm,),
        in_specs=[row, _full((1, d)), _full(w_qkv.shape)],
        out_specs=[row, row, pl.BlockSpec((tm, 3 * d), lambda i: (i, 0))],
        compiler_params=_params("parallel"),
        name="qkv_rows",
    )(x, gpre, w_qkv)


def _qkv_cols_kernel(x_ref, gpre_ref, wq_ref, wkt_ref, wvt_ref, q_ref, kt_ref, vt_ref):
    h = _rms(x_ref[...], gpre_ref[...]).astype(BF16)
    q_ref[...] = (_dot(h, wq_ref[...]) * Q_SCALE).astype(BF16)
    nt = (((1,), (1,)), ((), ()))
    kt_ref[...] = lax.dot_general(wkt_ref[...], h, nt, preferred_element_type=F32)
    vt_ref[...] = lax.dot_general(wvt_ref[...], h, nt, preferred_element_type=F32)


def _qkv_cols(x, gpre, wq, wkt, wvt, *, batch, tm):
    t, d = x.shape
    seq = t // batch
    per = seq // tm
    row = pl.BlockSpec((tm, d), lambda i: (i, 0))
    col = pl.BlockSpec((None, d, tm), lambda i: (i // per, 0, i % per))
    return pl.pallas_call(
        _qkv_cols_kernel,
        out_shape=[jax.ShapeDtypeStruct((t, d), BF16),
                   jax.ShapeDtypeStruct((batch, d, seq), F32), jax.ShapeDtypeStruct((batch, d, seq), F32)],
        grid=(t // tm,),
        in_specs=[row, _full((1, d)), _full(wq.shape), _full(wkt.shape), _full(wvt.shape)],
        out_specs=[row, col, col],
        compiler_params=_params("parallel"),
        name="qkv_cols",
    )(x, gpre, wq, wkt, wvt)


ATTN_PAIRS = 2
LATER_DONE = 160.0
SOFTPLUS_LINEAR = 64.0


def _suffix_matrix(tk):
    j = lax.broadcasted_iota(jnp.int32, (tk, tk), 0)
    s = lax.broadcasted_iota(jnp.int32, (tk, tk), 1)
    return jnp.where(j > s, 1.0, 0.0).astype(BF16)


def _softplus2(z):
    return jnp.where(z > SOFTPLUS_LINEAR, z, jnp.log(1.0 + jnp.exp2(z)) * 1.4426950408889634)


def _sb_run(qs, ks, vs, later_ref, acc_ref, suffix, diagonal, transposed, stack):
    nt = (((1,), (1,)), ((), ()))
    n = len(qs)
    sub = suffix.shape[0]
    if transposed:
        zs = [_dot(q, k) for q, k in zip(qs, ks)]
    else:
        zs = [lax.dot_general(q, k, nt, preferred_element_type=F32) for q, k in zip(qs, ks)]
    tq, keys = zs[0].shape
    m = keys // sub
    sps = [_softplus2(z) for z in zs]
    if diagonal:
        col = lax.broadcasted_iota(jnp.int32, (tq, keys), 1)
        causal = col < lax.broadcasted_iota(jnp.int32, (tq, keys), 0) + (m - 1) * sub
        sps = [jnp.where(causal, sp, 0.0) for sp in sps]
    spb = [sp.astype(BF16) for sp in sps]
    if stack:
        order = [(j, c) for j in range(m) for c in range(n)]
        cs_all = _dot(jnp.concatenate([spb[c][:, j * sub:(j + 1) * sub] for j, c in order], axis=0), suffix)
        css = [[cs_all[(j * n + c) * tq:(j * n + c + 1) * tq] for c in range(n)] for j in range(m)]
    else:
        css = [[_dot(spb[c][:, j * sub:(j + 1) * sub], suffix) for c in range(n)] for j in range(m)]
    aa = []
    for c in range(n):
        later = later_ref[c]
        offs = [None] * m
        for j in reversed(range(m)):
            offs[j] = css[j][c] + jnp.concatenate([later] * (sub // LANES), axis=1)
            total = css[j][c][:, 0:1] + sps[c][:, j * sub:j * sub + 1]
            later = later + jnp.broadcast_to(total, later.shape)
        later_ref[c] = later
        a = jnp.exp2(zs[c] - sps[c] - jnp.concatenate(offs, axis=1))
        if diagonal:
            a = jnp.where(causal, a, 0.0)
        aa.append(a.astype(BF16))
    if transposed:
        pv = [lax.dot_general(a, v, nt, preferred_element_type=F32) for a, v in zip(aa, vs)]
    else:
        pv = [_dot(a, v) for a, v in zip(aa, vs)]
    for c in range(n):
        acc_ref[c] += pv[c]


def _sb_heads(q, later_ref, acc_ref, stack=False):
    tq = q.shape[0]
    pairs = q.shape[1] // LANES
    suffix = _suffix_matrix(ATTN_BLOCK)
    first = lax.broadcasted_iota(jnp.int32, (tq, LANES), 1) < HEAD_DIM
    zero = jnp.zeros((tq, LANES), q.dtype)
    qs = []
    for p in range(pairs):
        q2 = q[:, p * LANES:(p + 1) * LANES]
        qs += [jnp.where(first, q2, zero), jnp.where(first, zero, q2)]
    later_ref[...] = jnp.zeros_like(later_ref)
    acc_ref[...] = jnp.zeros_like(acc_ref)

    def run(kv_of_pair, diagonal=False, transposed=True):
        kvs = [kv for kv in map(kv_of_pair, range(pairs)) for _ in range(2)]
        _sb_run(qs, [kv[0] for kv in kvs], [kv[1] for kv in kvs], later_ref, acc_ref, suffix, diagonal,
                transposed, stack)

    def result():
        return jnp.concatenate([jnp.where(first, acc_ref[2 * p], acc_ref[2 * p + 1]) for p in range(pairs)], axis=1)

    return run, result


def _attn_prompt_kernel(q_ref, kt_ref, vt_ref, o_ref, later_ref, acc_ref):
    blk = ATTN_BLOCK
    qi = pl.program_id(2)
    run, result = _sb_heads(q_ref[...], later_ref, acc_ref)

    def kv(kb, nblk):
        keys = pl.ds(pl.multiple_of(kb * blk, blk), nblk * blk)
        return lambda p: (kt_ref[p * LANES:(p + 1) * LANES, keys].astype(BF16),
                          vt_ref[p * LANES:(p + 1) * LANES, keys].astype(BF16))

    def unfinished():
        return jnp.min(later_ref[...]) < LATER_DONE

    @pl.when(qi == 0)
    def _():
        run(kv(0, 1), diagonal=True)

    @pl.when(qi > 0)
    def _():
        run(kv(qi - 1, 2), diagonal=True)

        def older(carry):
            kb, _ = carry
            run(kv(kb, 1))
            return kb - 1, unfinished()

        lax.while_loop(lambda c: (c[0] >= 0) & c[1], older, (qi - 2, unfinished()))

    o_ref[...] = result().astype(BF16)


def _attn_prompt(q, kt, vt):
    t, d = q.shape
    batch, _, seq = kt.shape
    blk, width = ATTN_BLOCK, ATTN_PAIRS * LANES
    nq = seq // blk
    kvblk = pl.BlockSpec((None, width, seq), lambda b, p, i: (b, p, 0))
    qblk = pl.BlockSpec((blk, width), lambda b, p, i: (b * nq + i, p))
    chains = pltpu.VMEM((2 * ATTN_PAIRS, blk, LANES), F32)
    return pl.pallas_call(
        _attn_prompt_kernel,
        out_shape=jax.ShapeDtypeStruct((t, d), BF16),
        grid=(batch, d // width, nq),
        in_specs=[qblk, kvblk, kvblk],
        out_specs=qblk,
        scratch_shapes=[chains, chains],
        compiler_params=_params("parallel", "parallel", "arbitrary"),
        name="attn_prompt",
    )(q, kt, vt)


def _attn_sample_kernel(q_ref, kn_ref, vn_ref, kct_ref, vct_ref, o_ref, later_ref, acc_ref):
    tq = q_ref.shape[0]
    run, result = _sb_heads(q_ref[...], later_ref, acc_ref, stack=True)
    pad = jnp.zeros((ATTN_BLOCK - tq, LANES), BF16)

    def own(p):
        cols = slice(p * LANES, (p + 1) * LANES)
        return (jnp.concatenate([kn_ref[:, cols], pad], axis=0), jnp.concatenate([vn_ref[:, cols], pad], axis=0))

    def cache(p):
        rows = slice(p * LANES, (p + 1) * LANES)
        return kct_ref[rows, :].astype(BF16), vct_ref[rows, :].astype(BF16)

    run(own, diagonal=True, transposed=False)
    run(cache)
    o_ref[...] = result().astype(BF16)


def _attn_sample(qkv, cache_kt, cache_vt, *, seq):
    t = qkv.shape[0]
    d = qkv.shape[1] // 3
    batch, _, past = cache_kt.shape
    width = ATTN_PAIRS * LANES
    ngrp = d // width
    cblk = pl.BlockSpec((None, width, past), lambda b, p: (b, p, 0))
    chains = pltpu.VMEM((2 * ATTN_PAIRS, seq, LANES), F32)
    return pl.pallas_call(
        _attn_sample_kernel,
        out_shape=jax.ShapeDtypeStruct((t, d), BF16),
        grid=(batch, ngrp),
        in_specs=[
            pl.BlockSpec((seq, width), lambda b, p: (b, p)),
            pl.BlockSpec((seq, width), lambda b, p: (b, ngrp + p)),
            pl.BlockSpec((seq, width), lambda b, p: (b, 2 * ngrp + p)),
            cblk, cblk,
        ],
        out_specs=pl.BlockSpec((seq, width), lambda b, p: (b, p)),
        scratch_shapes=[chains, chains],
        compiler_params=_params("parallel", "parallel"),
        name="attn_sample",
    )(qkv, qkv, qkv, cache_kt, cache_vt)


def _features_first(kv):
    b, s, h, e = kv.shape
    return jnp.transpose(kv, (0, 2, 3, 1)).reshape(b, h * e, s)


def _tokens_first(kvt, heads):
    b, d, s = kvt.shape
    return jnp.transpose(kvt.reshape(b, heads, d // heads, s), (0, 3, 1, 2))[None]


def _trunk(x, prev_pool, cache_kv, w, *, is_prompt, tm_mix, tm_tok):
    batch, seq, d = x.shape
    heads = d // HEAD_DIM
    t = batch * seq
    x = x.reshape(t, d)
    prev = jnp.pad(prev_pool, ((0, 0), (POOL_HALO - prev_pool.shape[1], 0), (0, 0)))
    chunk = 2 * CHUNK if is_prompt else seq
    outs = _mixer_ab(x, prev, w["norm_mix_pre"][0], w["norm_mix_post"][0], w["w_in_ab"], w["ln_v_g"],
                     w["ln_v_b"], w["w_spatial"], w["b_spatial"][:, :chunk].T, w["w_pool_map"],
                     w["pool_scale"], w["w_out_ab"], batch=batch, tm=tm_mix, chunk=chunk,
                     start_pos=0 if is_prompt else cache_kv[0].shape[1], emit_vn=not is_prompt)
    x, tail = outs[0], outs[1]
    x = _ffn(x, w["norm_ffn_pre"][0], w["norm_ffn_post"][0], w["w_gate"][0], w["w_up"][0], w["w_down"][0],
             tm=tm_tok)
    if is_prompt:
        q, kt, vt = _qkv_cols(x, w["norm_mix_pre"][1], w["w_q"], w["w_kt"], w["w_vt"], batch=batch, tm=tm_tok)
        o = _attn_prompt(q, kt, vt)
        k, v = _tokens_first(kt, heads), _tokens_first(vt, heads)
    else:
        k, v, qkv = _qkv_rows(x, w["norm_mix_pre"][1], w["w_qkv"], tm=tm_tok)
        o = _attn_sample(qkv, _features_first(cache_kv[0]), _features_first(cache_kv[1]), seq=seq)
        k, v = k.reshape(1, batch, seq, heads, HEAD_DIM), v.reshape(1, batch, seq, heads, HEAD_DIM)
    x = _ffn(x, w["norm_ffn_pre"][1], w["norm_ffn_post"][1], w["w_gate"][1], w["w_up"][1], w["w_down"][1],
             tm=tm_tok, proj=(o, w["w_o_sb"], w["norm_mix_post"][1]))
    res = [x.reshape(batch, seq, d), tail[None, :, 1:, :], k, v]
    if not is_prompt:
        res.append(outs[2].reshape(1, batch, seq, -1))
    return res


def kernel(x_prompt, x_sample, state_pool, cache_k, cache_v, norm_mix_pre, norm_mix_post, norm_ffn_pre,
           norm_ffn_post, w_in_ab, ln_v_g, ln_v_b, w_spatial, b_spatial, w_pool_map, pool_scale, w_out_ab,
           w_qkv, w_o_sb, w_gate, w_up, w_down):
    depth, d = norm_mix_pre.shape
    assert depth == 2 and w_in_ab.shape[0] == 1 and w_qkv.shape[0] == 1
    wqkv = w_qkv[0].astype(BF16)
    w = dict(
        norm_mix_pre=norm_mix_pre.reshape(depth, 1, d), norm_mix_post=norm_mix_post.reshape(depth, 1, d),
        norm_ffn_pre=norm_ffn_pre.reshape(depth, 1, d), norm_ffn_post=norm_ffn_post.reshape(depth, 1, d),
        w_in_ab=w_in_ab[0].astype(BF16), ln_v_g=ln_v_g, ln_v_b=ln_v_b, w_spatial=w_spatial[0],
        b_spatial=b_spatial[0], w_pool_map=w_pool_map[0].astype(BF16), pool_scale=pool_scale,
        w_out_ab=w_out_ab[0].astype(BF16), w_qkv=wqkv, w_q=wqkv[:, :d], w_kt=wqkv[:, d:2 * d].T,
        w_vt=wqkv[:, 2 * d:].T, w_o_sb=w_o_sb[0].astype(BF16),
        w_gate=w_gate.astype(BF16), w_up=w_up.astype(BF16), w_down=w_down.astype(BF16),
    )
    batch = x_prompt.shape[0]
    zero_pool = jnp.zeros((batch,) + state_pool.shape[2:], F32)
    y_p, pool_p, k_p, v_p = _trunk(x_prompt, zero_pool, None, w, is_prompt=True, tm_mix=1024, tm_tok=512)
    y_s, pool_s, k_s, v_s, vn_s = _trunk(x_sample, state_pool[0], (cache_k[0], cache_v[0]), w,
                                         is_prompt=False, tm_mix=x_sample.shape[1],
                                         tm_tok=x_sample.shape[0] * x_sample.shape[1])
    return (y_p, y_s, pool_p, k_p, v_p, pool_s, k_s, v_s, vn_s)
```

```python
import functools

import jax
import jax.numpy as jnp
from jax import lax
from jax.experimental import pallas as pl
from jax.experimental.pallas import tpu as pltpu

F32 = jnp.float32
BF16 = jnp.bfloat16

EPS = 1e-6
CHUNK = 64
POOL_WINDOWS = (2, 4, 8, 16)
POOL_HALO = 16
LANES = 128
HEAD_DIM = 64
ATTN_BLOCK = 256
VMEM_LIMIT = 56 * 1024 * 1024


def _rms(x, g):
    return x * lax.rsqrt(jnp.mean(x * x, axis=-1, keepdims=True) + EPS) * g


def _dot(a, b):
    return jnp.dot(a, b, preferred_element_type=F32)


def _params(*sem):
    return pltpu.CompilerParams(dimension_semantics=sem, vmem_limit_bytes=VMEM_LIMIT)


def _full(shape):
    return pl.BlockSpec(shape, lambda *_: (0,) * len(shape))


def _layer_row(per_layer, layer):
    return pl.BlockSpec((None,) + per_layer.shape[1:], lambda *_: (layer, 0, 0))


def _mixer_ab_kernel(x_ref, prev_ref, gpre_ref, gpost_ref, win_ref, lng_ref, lnb_ref, ws_ref,
                     bs_ref, wmap_ref, pscale_ref, wout_ref, *rest,
                     tm, chunk, tiles_per_batch, start_pos, emit_vn):
    if emit_vn:
        x1_ref, tail_ref, vn_ref, ext_ref, cat_ref = rest
    else:
        x1_ref, tail_ref, ext_ref, cat_ref = rest
    aw = 4 * LANES
    tib = pl.program_id(0) % tiles_per_batch

    @pl.when(tib == 0)
    def _():
        ext_ref[0:POOL_HALO, :] = prev_ref[...]

    x = x_ref[...]
    h = _rms(x, gpre_ref[...]).astype(BF16)
    z = _dot(h, win_ref[...])
    zuv = z[:, :2 * aw]
    uv = 0.5 * zuv * (1.0 + lax.erf(zuv * (0.5 ** 0.5)))
    u, v = uv[:, :aw], uv[:, aw:]
    mu = jnp.mean(v, axis=-1, keepdims=True)
    vc = v - mu
    vn = vc * lax.rsqrt(jnp.mean(vc * vc, axis=-1, keepdims=True) + EPS) * lng_ref[...] + lnb_ref[...]
    if emit_vn:
        vn_ref[...] = vn

    ri = lax.broadcasted_iota(jnp.int32, (chunk, LANES), 0)
    ci = lax.broadcasted_iota(jnp.int32, (chunk, LANES), 1)
    allowed = (ci // CHUNK <= ri // CHUNK) & (ci < chunk)
    bias = bs_ref[...]
    vnb = vn.astype(BF16)
    for g in range(4):
        wg = jnp.where(allowed, ws_ref[g, 0:chunk, :], 0.0).astype(BF16)
        bg = bias[:, g:g + 1]
        for c in range(tm // chunk):
            vblk = vnb[c * chunk:(c + 1) * chunk, g * LANES:(g + 1) * LANES]
            if chunk < LANES:
                vblk = jnp.concatenate([vblk, jnp.zeros((LANES - chunk, LANES), BF16)], axis=0)
            s = _dot(wg, vblk) + bg
            ublk = u[c * chunk:(c + 1) * chunk, g * LANES:(g + 1) * LANES]
            cat_ref[c * chunk:(c + 1) * chunk, g * LANES:(g + 1) * LANES] = (ublk * s).astype(BF16)

    p = z[:, 2 * aw:]
    ext_ref[POOL_HALO:POOL_HALO + tm, :] = p
    e1 = ext_ref[...]
    e2 = e1 + pltpu.roll(e1, 1, axis=0)
    e4 = e2[:, LANES:] + pltpu.roll(e2[:, LANES:], 2, axis=0)
    e8 = e4[:, LANES:] + pltpu.roll(e4[:, LANES:], 4, axis=0)
    e16 = e8[:, LANES:] + pltpu.roll(e8[:, LANES:], 8, axis=0)
    wins = (e2[POOL_HALO:, :LANES], e4[POOL_HALO:, :LANES], e8[POOL_HALO:, :LANES], e16[POOL_HALO:, :])
    pos = start_pos + tib * tm + lax.broadcasted_iota(jnp.int32, (tm, 1), 0)
    pscale = pscale_ref[...]
    for g, w in enumerate(POOL_WINDOWS):
        inv_cnt = 1.0 / jnp.minimum(w, pos + 1).astype(F32)
        d = wins[g] * inv_cnt - p[:, g * LANES:(g + 1) * LANES]
        yb = _dot(d.astype(BF16), wmap_ref[g]) * pscale[:, g * LANES:(g + 1) * LANES]
        cat_ref[:, aw + g * LANES:aw + (g + 1) * LANES] = yb.astype(BF16)

    tail = ext_ref[tm:tm + POOL_HALO, :]
    ext_ref[0:POOL_HALO, :] = tail

    @pl.when(tib == tiles_per_batch - 1)
    def _():
        tail_ref[...] = tail

    y = _dot(cat_ref[...], wout_ref[...])
    x1_ref[...] = x + _rms(y, gpost_ref[...])


def _mixer_ab(x, prev, gpre, gpost, w_in, ln_g, ln_b, w_s, b_s, w_map, p_scale, w_out,
              *, batch, tm, chunk, start_pos, emit_vn, layer):
    t, d = x.shape
    aw = 4 * LANES
    tiles_per_batch = t // batch // tm
    kern = functools.partial(_mixer_ab_kernel, tm=tm, chunk=chunk, tiles_per_batch=tiles_per_batch,
                             start_pos=start_pos, emit_vn=emit_vn)
    out_shape = [jax.ShapeDtypeStruct((t, d), F32), jax.ShapeDtypeStruct((batch, POOL_HALO, aw), F32)]
    out_specs = [pl.BlockSpec((tm, d), lambda i: (i, 0)),
                 pl.BlockSpec((None, POOL_HALO, aw), lambda i: (i // tiles_per_batch, 0, 0))]
    if emit_vn:
        out_shape.append(jax.ShapeDtypeStruct((t, aw), F32))
        out_specs.append(pl.BlockSpec((tm, aw), lambda i: (i, 0)))
    return pl.pallas_call(
        kern,
        out_shape=out_shape,
        grid=(t // tm,),
        in_specs=[
            pl.BlockSpec((tm, d), lambda i: (i, 0)),
            pl.BlockSpec((None, POOL_HALO, aw), lambda i: (i // tiles_per_batch, 0, 0)),
            _layer_row(gpre, layer), _layer_row(gpost, layer), _full(w_in.shape), _full((1, aw)), _full((1, aw)),
            _full(w_s.shape), _full(b_s.shape), _full(w_map.shape), _full((1, aw)), _full(w_out.shape),
        ],
        out_specs=out_specs,
        scratch_shapes=[pltpu.VMEM((tm + POOL_HALO, aw), F32), pltpu.VMEM((tm, 2 * aw), BF16)],
        compiler_params=_params("arbitrary"),
        name="mixer_ab",
    )(x, prev, gpre, gpost, w_in, ln_g, ln_b, w_s, b_s, w_map, p_scale, w_out)


def _ffn_kernel(*refs, ff_chunk, has_proj):
    if has_proj:
        x_ref, o_ref, wo_ref, gmix_ref, gpre_ref, gpost_ref, wg_ref, wu_ref, wd_ref, out_ref = refs
        x = x_ref[...] + _rms(_dot(o_ref[...], wo_ref[...]), gmix_ref[...])
    else:
        x_ref, gpre_ref, gpost_ref, wg_ref, wu_ref, wd_ref, out_ref = refs
        x = x_ref[...]
    h = _rms(x, gpre_ref[...]).astype(BF16)
    d_ff = wg_ref.shape[1]
    f = None
    for c0 in range(0, d_ff, ff_chunk):
        c1 = min(c0 + ff_chunk, d_ff)
        act = jax.nn.silu(_dot(h, wg_ref[:, c0:c1])) * _dot(h, wu_ref[:, c0:c1])
        part = _dot(act.astype(BF16), wd_ref[c0:c1, :])
        f = part if f is None else f + part
    out_ref[...] = x + _rms(f, gpost_ref[...])


def _ffn(x, gpre, gpost, wg, wu, wd, *, tm, layer, proj=None):
    t, d = x.shape
    row = pl.BlockSpec((tm, d), lambda i: (i, 0))
    args, specs = [x], [row]
    if proj is not None:
        o, wo, gmix = proj
        args += [o, wo, gmix]
        specs += [row, _full(wo.shape), _layer_row(gmix, layer)]
    args += [gpre, gpost, wg, wu, wd]
    specs += [_layer_row(gpre, layer), _layer_row(gpost, layer), _full(wg.shape), _full(wu.shape), _full(wd.shape)]
    return pl.pallas_call(
        functools.partial(_ffn_kernel, ff_chunk=512, has_proj=proj is not None),
        out_shape=jax.ShapeDtypeStruct((t, d), F32),
        grid=(t // tm,),
        in_specs=specs,
        out_specs=row,
        compiler_params=_params("parallel"),
        name="ffn_proj" if proj is not None else "ffn",
    )(*args)


Q_SCALE = HEAD_DIM ** -0.5 * 1.4426950408889634


def _qkv_rows_kernel(x_ref, gpre_ref, w_ref, k_ref, v_ref, qkv_ref):
    d = x_ref.shape[1]
    h = _rms(x_ref[...], gpre_ref[...]).astype(BF16)
    qkv = _dot(h, w_ref[...])
    k_ref[...] = qkv[:, d:2 * d]
    v_ref[...] = qkv[:, 2 * d:]
    qkv_ref[:, :d] = (qkv[:, :d] * Q_SCALE).astype(BF16)
    qkv_ref[:, d:] = qkv[:, d:].astype(BF16)


def _qkv_rows(x, gpre, w_qkv, *, tm, layer):
    t, d = x.shape
    row = pl.BlockSpec((tm, d), lambda i: (i, 0))
    return pl.pallas_call(
        _qkv_rows_kernel,
        out_shape=[jax.ShapeDtypeStruct((t, d), F32), jax.ShapeDtypeStruct((t, d), F32),
                   jax.ShapeDtypeStruct((t, 3 * d), BF16)],
        grid=(t // tm,),
        in_specs=[row, _layer_row(gpre, layer), _full(w_qkv.shape)],
        out_specs=[row, row, pl.BlockSpec((tm, 3 * d), lambda i: (i, 0))],
        compiler_params=_params("parallel"),
        name="qkv_rows",
    )(x, gpre, w_qkv)


def _qkv_cols_kernel(x_ref, gpre_ref, wq_ref, wkt_ref, wvt_ref, q_ref, kt_ref, vt_ref, ktb_ref, vtb_ref):
    h = _rms(x_ref[...], gpre_ref[...]).astype(BF16)
    q_ref[...] = (_dot(h, wq_ref[...]) * Q_SCALE).astype(BF16)
    nt = (((1,), (1,)), ((), ()))
    kt = lax.dot_general(wkt_ref[...], h, nt, preferred_element_type=F32)
    kt_ref[...] = kt
    ktb_ref[...] = kt.astype(BF16)
    vt = lax.dot_general(wvt_ref[...], h, nt, preferred_element_type=F32)
    vt_ref[...] = vt
    vtb_ref[...] = vt.astype(BF16)


def _qkv_cols(x, gpre, wq, wkt, wvt, *, batch, tm, layer):
    t, d = x.shape
    seq = t // batch
    per = seq // tm
    row = pl.BlockSpec((tm, d), lambda i: (i, 0))
    col = pl.BlockSpec((None, d, tm), lambda i: (i // per, 0, i % per))
    return pl.pallas_call(
        _qkv_cols_kernel,
        out_shape=[jax.ShapeDtypeStruct((t, d), BF16),
                   jax.ShapeDtypeStruct((batch, d, seq), F32), jax.ShapeDtypeStruct((batch, d, seq), F32),
                   jax.ShapeDtypeStruct((batch, d, seq), BF16), jax.ShapeDtypeStruct((batch, d, seq), BF16)],
        grid=(t // tm,),
        in_specs=[row, _layer_row(gpre, layer), _full(wq.shape), _full(wkt.shape), _full(wvt.shape)],
        out_specs=[row, col, col, col, col],
        compiler_params=_params("parallel"),
        name="qkv_cols",
    )(x, gpre, wq, wkt, wvt)


ATTN_PAIRS = 2
LATER_DONE = 160.0
SOFTPLUS_LINEAR = 64.0


def _suffix_matrix(tk):
    j = lax.broadcasted_iota(jnp.int32, (tk, tk), 0)
    s = lax.broadcasted_iota(jnp.int32, (tk, tk), 1)
    return jnp.where(j > s, 1.0, 0.0).astype(BF16)


def _softplus2(z):
    return jnp.where(z > SOFTPLUS_LINEAR, z, jnp.log(1.0 + jnp.exp2(z)) * 1.4426950408889634)


def _sb_run(qs, ks, vs, later_ref, acc_ref, suffix, diagonal, transposed, stack):
    nt = (((1,), (1,)), ((), ()))
    n = len(qs)
    sub = suffix.shape[0]
    if transposed:
        zs = [_dot(q, k) for q, k in zip(qs, ks)]
    else:
        zs = [lax.dot_general(q, k, nt, preferred_element_type=F32) for q, k in zip(qs, ks)]
    tq, keys = zs[0].shape
    m = keys // sub
    sps = [_softplus2(z) for z in zs]
    if diagonal:
        col = lax.broadcasted_iota(jnp.int32, (tq, keys), 1)
        causal = col < lax.broadcasted_iota(jnp.int32, (tq, keys), 0) + (m - 1) * sub
        sps = [jnp.where(causal, sp, 0.0) for sp in sps]
    spb = [sp.astype(BF16) for sp in sps]
    if stack:
        order = [(j, c) for j in range(m) for c in range(n)]
        cs_all = _dot(jnp.concatenate([spb[c][:, j * sub:(j + 1) * sub] for j, c in order], axis=0), suffix)
        css = [[cs_all[(j * n + c) * tq:(j * n + c + 1) * tq] for c in range(n)] for j in range(m)]
    else:
        css = [[_dot(spb[c][:, j * sub:(j + 1) * sub], suffix) for c in range(n)] for j in range(m)]
    aa = []
    for c in range(n):
        later = later_ref[c]
        offs = [None] * m
        for j in reversed(range(m)):
            offs[j] = css[j][c] + jnp.concatenate([later] * (sub // LANES), axis=1)
            total = css[j][c][:, 0:1] + sps[c][:, j * sub:j * sub + 1]
            later = later + jnp.broadcast_to(total, later.shape)
        later_ref[c] = later
        a = jnp.exp2(zs[c] - sps[c] - jnp.concatenate(offs, axis=1))
        if diagonal:
            a = jnp.where(causal, a, 0.0)
        aa.append(a.astype(BF16))
    if transposed:
        pv = [lax.dot_general(a, v, nt, preferred_element_type=F32) for a, v in zip(aa, vs)]
    else:
        pv = [_dot(a, v) for a, v in zip(aa, vs)]
    for c in range(n):
        acc_ref[c] += pv[c]


def _sb_heads(q, later_ref, acc_ref, stack=False):
    tq = q.shape[0]
    pairs = q.shape[1] // LANES
    suffix = _suffix_matrix(ATTN_BLOCK)
    first = lax.broadcasted_iota(jnp.int32, (tq, LANES), 1) < HEAD_DIM
    zero = jnp.zeros((tq, LANES), q.dtype)
    qs = []
    for p in range(pairs):
        q2 = q[:, p * LANES:(p + 1) * LANES]
        qs += [jnp.where(first, q2, zero), jnp.where(first, zero, q2)]
    later_ref[...] = jnp.zeros_like(later_ref)
    acc_ref[...] = jnp.zeros_like(acc_ref)

    def run(kv_of_pair, diagonal=False, transposed=True):
        kvs = [kv for kv in map(kv_of_pair, range(pairs)) for _ in range(2)]
        _sb_run(qs, [kv[0] for kv in kvs], [kv[1] for kv in kvs], later_ref, acc_ref, suffix, diagonal,
                transposed, stack)

    def result():
        return jnp.concatenate([jnp.where(first, acc_ref[2 * p], acc_ref[2 * p + 1]) for p in range(pairs)], axis=1)

    return run, result


def _attn_prompt_kernel(q_ref, kt_ref, vt_ref, o_ref, later_ref, acc_ref):
    blk = ATTN_BLOCK

    def kv(kb, nblk):
        keys = pl.ds(pl.multiple_of(kb * blk, blk), nblk * blk)
        return lambda p: (kt_ref[p * LANES:(p + 1) * LANES, keys], vt_ref[p * LANES:(p + 1) * LANES, keys])

    def unfinished():
        return jnp.min(later_ref[...]) < LATER_DONE

    def query_block(qi, is_first):
        rows = pl.ds(pl.multiple_of(qi * blk, blk), blk)
        run, result = _sb_heads(q_ref[rows, :], later_ref, acc_ref)
        if is_first:
            run(kv(0, 1), diagonal=True)
        else:
            run(kv(qi - 1, 2), diagonal=True)

            def older(carry):
                kb, _ = carry
                run(kv(kb, 1))
                return kb - 1, unfinished()

            lax.while_loop(lambda c: (c[0] >= 0) & c[1], older, (qi - 2, unfinished()))
        o_ref[rows, :] = result().astype(BF16)

    query_block(0, True)
    pl.loop(1, q_ref.shape[0] // blk)(lambda qi: query_block(qi, False))


def _attn_prompt(q, kt, vt):
    t, d = q.shape
    batch, _, seq = kt.shape
    width = ATTN_PAIRS * LANES
    kvblk = pl.BlockSpec((None, width, seq), lambda b, p: (b, p, 0))
    qblk = pl.BlockSpec((seq, width), lambda b, p: (b, p))
    chains = pltpu.VMEM((2 * ATTN_PAIRS, ATTN_BLOCK, LANES), F32)
    return pl.pallas_call(
        _attn_prompt_kernel,
        out_shape=jax.ShapeDtypeStruct((t, d), BF16),
        grid=(batch, d // width),
        in_specs=[qblk, kvblk, kvblk],
        out_specs=qblk,
        scratch_shapes=[chains, chains],
        compiler_params=_params("parallel", "parallel"),
        name="attn_prompt",
    )(q, kt, vt)


def _attn_sample_kernel(q_ref, kn_ref, vn_ref, kct_ref, vct_ref, o_ref, later_ref, acc_ref):
    tq = q_ref.shape[0]
    run, result = _sb_heads(q_ref[...], later_ref, acc_ref, stack=True)
    pad = jnp.zeros((ATTN_BLOCK - tq, LANES), BF16)

    def own(p):
        cols = slice(p * LANES, (p + 1) * LANES)
        return (jnp.concatenate([kn_ref[:, cols], pad], axis=0), jnp.concatenate([vn_ref[:, cols], pad], axis=0))

    def cache(lo, hi):
        def kv(p):
            rows = slice(p * LANES, (p + 1) * LANES)
            return kct_ref[rows, lo:hi].astype(BF16), vct_ref[rows, lo:hi].astype(BF16)
        return kv

    past = kct_ref.shape[1]
    recent = past - 2 * ATTN_BLOCK
    run(own, diagonal=True, transposed=False)
    run(cache(recent, past))

    @pl.when(jnp.min(later_ref[...]) < LATER_DONE)
    def _():
        run(cache(0, recent))

    o_ref[...] = result().astype(BF16)


def _attn_sample(qkv, cache_kt, cache_vt, *, seq):
    t = qkv.shape[0]
    d = qkv.shape[1] // 3
    batch, _, past = cache_kt.shape
    width = ATTN_PAIRS * LANES
    ngrp = d // width
    cblk = pl.BlockSpec((None, width, past), lambda b, p: (b, p, 0))
    chains = pltpu.VMEM((2 * ATTN_PAIRS, seq, LANES), F32)
    return pl.pallas_call(
        _attn_sample_kernel,
        out_shape=jax.ShapeDtypeStruct((t, d), BF16),
        grid=(batch, ngrp),
        in_specs=[
            pl.BlockSpec((seq, width), lambda b, p: (b, p)),
            pl.BlockSpec((seq, width), lambda b, p: (b, ngrp + p)),
            pl.BlockSpec((seq, width), lambda b, p: (b, 2 * ngrp + p)),
            cblk, cblk,
        ],
        out_specs=pl.BlockSpec((seq, width), lambda b, p: (b, p)),
        scratch_shapes=[chains, chains],
        compiler_params=_params("parallel", "parallel"),
        name="attn_sample",
    )(qkv, qkv, qkv, cache_kt, cache_vt)


def _features_first(kv):
    b, s, h, e = kv.shape
    return jnp.transpose(kv, (0, 2, 3, 1)).reshape(b, h * e, s)


def _tokens_first(kvt, heads):
    b, d, s = kvt.shape
    return jnp.transpose(kvt.reshape(b, heads, d // heads, s), (0, 3, 1, 2))[None]


def _trunk(x, prev_pool, cache_kv, w, *, is_prompt, tm_mix, tm_tok):
    batch, seq, d = x.shape
    heads = d // HEAD_DIM
    t = batch * seq
    x = x.reshape(t, d)
    prev = jnp.pad(prev_pool, ((0, 0), (POOL_HALO - prev_pool.shape[1], 0), (0, 0)))
    chunk = 2 * CHUNK if is_prompt else seq
    outs = _mixer_ab(x, prev, w["norm_mix_pre"], w["norm_mix_post"], w["w_in_ab"], w["ln_v_g"],
                     w["ln_v_b"], w["w_spatial"], w["b_spatial"][:, :chunk].T, w["w_pool_map"],
                     w["pool_scale"], w["w_out_ab"], batch=batch, tm=tm_mix, chunk=chunk,
                     start_pos=0 if is_prompt else cache_kv[0].shape[1], emit_vn=not is_prompt, layer=0)
    x, tail = outs[0], outs[1]
    x = _ffn(x, w["norm_ffn_pre"], w["norm_ffn_post"], w["w_gate"][0], w["w_up"][0], w["w_down"][0],
             tm=tm_tok, layer=0)
    if is_prompt:
        q, kt, vt, ktb, vtb = _qkv_cols(x, w["norm_mix_pre"], w["w_q"], w["w_kt"], w["w_vt"],
                                        batch=batch, tm=tm_tok, layer=1)
        o = _attn_prompt(q, ktb, vtb)
        k, v = _tokens_first(kt, heads), _tokens_first(vt, heads)
    else:
        k, v, qkv = _qkv_rows(x, w["norm_mix_pre"], w["w_qkv"], tm=tm_tok, layer=1)
        o = _attn_sample(qkv, _features_first(cache_kv[0]), _features_first(cache_kv[1]), seq=seq)
        k, v = k.reshape(1, batch, seq, heads, HEAD_DIM), v.reshape(1, batch, seq, heads, HEAD_DIM)
    x = _ffn(x, w["norm_ffn_pre"], w["norm_ffn_post"], w["w_gate"][1], w["w_up"][1], w["w_down"][1],
             tm=tm_tok, layer=1, proj=(o, w["w_o_sb"], w["norm_mix_post"]))
    res = [x.reshape(batch, seq, d), tail[None, :, 1:, :], k, v]
    if not is_prompt:
        res.append(outs[2].reshape(1, batch, seq, -1))
    return res


def kernel(x_prompt, x_sample, state_pool, cache_k, cache_v, norm_mix_pre, norm_mix_post, norm_ffn_pre,
           norm_ffn_post, w_in_ab, ln_v_g, ln_v_b, w_spatial, b_spatial, w_pool_map, pool_scale, w_out_ab,
           w_qkv, w_o_sb, w_gate, w_up, w_down):
    depth, d = norm_mix_pre.shape
    assert depth == 2 and w_in_ab.shape[0] == 1 and w_qkv.shape[0] == 1
    wqkv = w_qkv[0].astype(BF16)
    w = dict(
        norm_mix_pre=norm_mix_pre.reshape(depth, 1, d), norm_mix_post=norm_mix_post.reshape(depth, 1, d),
        norm_ffn_pre=norm_ffn_pre.reshape(depth, 1, d), norm_ffn_post=norm_ffn_post.reshape(depth, 1, d),
        w_in_ab=w_in_ab[0].astype(BF16), ln_v_g=ln_v_g, ln_v_b=ln_v_b, w_spatial=w_spatial[0],
        b_spatial=b_spatial[0], w_pool_map=w_pool_map[0].astype(BF16), pool_scale=pool_scale,
        w_out_ab=w_out_ab[0].astype(BF16), w_qkv=wqkv, w_q=wqkv[:, :d], w_kt=wqkv[:, d:2 * d].T,
        w_vt=wqkv[:, 2 * d:].T, w_o_sb=w_o_sb[0].astype(BF16),
        w_gate=[w_gate[l].astype(BF16) for l in range(depth)], w_up=[w_up[l].astype(BF16) for l in range(depth)],
        w_down=[w_down[l].astype(BF16) for l in range(depth)],
    )
    batch = x_prompt.shape[0]
    zero_pool = jnp.zeros((batch,) + state_pool.shape[2:], F32)
    y_p, pool_p, k_p, v_p = _trunk(x_prompt, zero_pool, None, w, is_prompt=True, tm_mix=1024, tm_tok=512)
    y_s, pool_s, k_s, v_s, vn_s = _trunk(x_sample, state_pool[0], (cache_k[0], cache_v[0]), w,
                                         is_prompt=False, tm_mix=x_sample.shape[1],
                                         tm_tok=x_sample.shape[0] * x_sample.shape[1])
    return (y_p, y_s, pool_p, k_p, v_p, pool_s, k_s, v_s, vn_s)
```

```python
import functools

import jax
import jax.numpy as jnp
from jax import lax
from jax.experimental import pallas as pl
from jax.experimental.pallas import tpu as pltpu

F32 = jnp.float32
BF16 = jnp.bfloat16

EPS = 1e-6
CHUNK = 64
POOL_WINDOWS = (2, 4, 8, 16)
POOL_HALO = 16
LANES = 128
HEAD_DIM = 64
ATTN_BLOCK = 256
VMEM_LIMIT = 56 * 1024 * 1024


def _rms(x, g):
    return x * lax.rsqrt(jnp.mean(x * x, axis=-1, keepdims=True) + EPS) * g


def _dot(a, b):
    return jnp.dot(a, b, preferred_element_type=F32)


def _params(*sem):
    return pltpu.CompilerParams(dimension_semantics=sem, vmem_limit_bytes=VMEM_LIMIT)


def _full(shape):
    return pl.BlockSpec(shape, lambda *_: (0,) * len(shape))


def _layer_row(per_layer, layer):
    return pl.BlockSpec((None,) + per_layer.shape[1:], lambda *_: (layer, 0, 0))


def _mixer_ab_kernel(x_ref, prev_ref, gpre_ref, gpost_ref, win_ref, lng_ref, lnb_ref, ws_ref,
                     bs_ref, wmap_ref, pscale_ref, wout_ref, *rest,
                     tm, chunk, tiles_per_batch, start_pos, emit_vn):
    if emit_vn:
        x1_ref, tail_ref, vn_ref, ext_ref, cat_ref = rest
    else:
        x1_ref, tail_ref, ext_ref, cat_ref = rest
    aw = 4 * LANES
    tib = pl.program_id(0) % tiles_per_batch

    @pl.when(tib == 0)
    def _():
        ext_ref[0:POOL_HALO, :] = prev_ref[...]

    x = x_ref[...]
    h = _rms(x, gpre_ref[...]).astype(BF16)
    z = _dot(h, win_ref[...])
    zuv = z[:, :2 * aw]
    uv = 0.5 * zuv * (1.0 + lax.erf(zuv * (0.5 ** 0.5)))
    u, v = uv[:, :aw], uv[:, aw:]
    mu = jnp.mean(v, axis=-1, keepdims=True)
    vc = v - mu
    vn = vc * lax.rsqrt(jnp.mean(vc * vc, axis=-1, keepdims=True) + EPS) * lng_ref[...] + lnb_ref[...]
    if emit_vn:
        vn_ref[...] = vn

    ri = lax.broadcasted_iota(jnp.int32, (chunk, LANES), 0)
    ci = lax.broadcasted_iota(jnp.int32, (chunk, LANES), 1)
    allowed = (ci // CHUNK <= ri // CHUNK) & (ci < chunk)
    bias = bs_ref[...]
    vnb = vn.astype(BF16)
    for g in range(4):
        wg = jnp.where(allowed, ws_ref[g, 0:chunk, :], 0.0).astype(BF16)
        bg = bias[:, g:g + 1]
        for c in range(tm // chunk):
            vblk = vnb[c * chunk:(c + 1) * chunk, g * LANES:(g + 1) * LANES]
            if chunk < LANES:
                vblk = jnp.concatenate([vblk, jnp.zeros((LANES - chunk, LANES), BF16)], axis=0)
            s = _dot(wg, vblk) + bg
            ublk = u[c * chunk:(c + 1) * chunk, g * LANES:(g + 1) * LANES]
            cat_ref[c * chunk:(c + 1) * chunk, g * LANES:(g + 1) * LANES] = (ublk * s).astype(BF16)

    p = z[:, 2 * aw:]
    ext_ref[POOL_HALO:POOL_HALO + tm, :] = p
    e1 = ext_ref[...]
    e2 = e1 + pltpu.roll(e1, 1, axis=0)
    e4 = e2[:, LANES:] + pltpu.roll(e2[:, LANES:], 2, axis=0)
    e8 = e4[:, LANES:] + pltpu.roll(e4[:, LANES:], 4, axis=0)
    e16 = e8[:, LANES:] + pltpu.roll(e8[:, LANES:], 8, axis=0)
    wins = (e2[POOL_HALO:, :LANES], e4[POOL_HALO:, :LANES], e8[POOL_HALO:, :LANES], e16[POOL_HALO:, :])
    pos = start_pos + tib * tm + lax.broadcasted_iota(jnp.int32, (tm, 1), 0)
    pscale = pscale_ref[...]
    for g, w in enumerate(POOL_WINDOWS):
        inv_cnt = 1.0 / jnp.minimum(w, pos + 1).astype(F32)
        d = wins[g] * inv_cnt - p[:, g * LANES:(g + 1) * LANES]
        yb = _dot(d.astype(BF16), wmap_ref[g]) * pscale[:, g * LANES:(g + 1) * LANES]
        cat_ref[:, aw + g * LANES:aw + (g + 1) * LANES] = yb.astype(BF16)

    tail = ext_ref[tm:tm + POOL_HALO, :]
    ext_ref[0:POOL_HALO, :] = tail

    @pl.when(tib == tiles_per_batch - 1)
    def _():
        tail_ref[...] = tail

    y = _dot(cat_ref[...], wout_ref[...])
    x1_ref[...] = x + _rms(y, gpost_ref[...])


def _mixer_ab(x, prev, gpre, gpost, w_in, ln_g, ln_b, w_s, b_s, w_map, p_scale, w_out,
              *, batch, tm, chunk, start_pos, emit_vn, layer):
    t, d = x.shape
    aw = 4 * LANES
    tiles_per_batch = t // batch // tm
    kern = functools.partial(_mixer_ab_kernel, tm=tm, chunk=chunk, tiles_per_batch=tiles_per_batch,
                             start_pos=start_pos, emit_vn=emit_vn)
    out_shape = [jax.ShapeDtypeStruct((t, d), F32), jax.ShapeDtypeStruct((batch, POOL_HALO, aw), F32)]
    out_specs = [pl.BlockSpec((tm, d), lambda i: (i, 0)),
                 pl.BlockSpec((None, POOL_HALO, aw), lambda i: (i // tiles_per_batch, 0, 0))]
    if emit_vn:
        out_shape.append(jax.ShapeDtypeStruct((t, aw), F32))
        out_specs.append(pl.BlockSpec((tm, aw), lambda i: (i, 0)))
    return pl.pallas_call(
        kern,
        out_shape=out_shape,
        grid=(t // tm,),
        in_specs=[
            pl.BlockSpec((tm, d), lambda i: (i, 0)),
            pl.BlockSpec((None, POOL_HALO, aw), lambda i: (i // tiles_per_batch, 0, 0)),
            _layer_row(gpre, layer), _layer_row(gpost, layer), _full(w_in.shape), _full((1, aw)), _full((1, aw)),
            _full(w_s.shape), _full(b_s.shape), _full(w_map.shape), _full((1, aw)), _full(w_out.shape),
        ],
        out_specs=out_specs,
        scratch_shapes=[pltpu.VMEM((tm + POOL_HALO, aw), F32), pltpu.VMEM((tm, 2 * aw), BF16)],
        compiler_params=_params("arbitrary"),
        name="mixer_ab",
    )(x, prev, gpre, gpost, w_in, ln_g, ln_b, w_s, b_s, w_map, p_scale, w_out)


def _ffn_kernel(*refs, ff_chunk, has_proj):
    if has_proj:
        x_ref, o_ref, wo_ref, gmix_ref, gpre_ref, gpost_ref, wg_ref, wu_ref, wd_ref, out_ref = refs
        x = x_ref[...] + _rms(_dot(o_ref[...], wo_ref[...]), gmix_ref[...])
    else:
        x_ref, gpre_ref, gpost_ref, wg_ref, wu_ref, wd_ref, out_ref = refs
        x = x_ref[...]
    h = _rms(x, gpre_ref[...]).astype(BF16)
    d_ff = wg_ref.shape[1]
    f = None
    for c0 in range(0, d_ff, ff_chunk):
        c1 = min(c0 + ff_chunk, d_ff)
        act = jax.nn.silu(_dot(h, wg_ref[:, c0:c1])) * _dot(h, wu_ref[:, c0:c1])
        part = _dot(act.astype(BF16), wd_ref[c0:c1, :])
        f = part if f is None else f + part
    out_ref[...] = x + _rms(f, gpost_ref[...])


def _ffn(x, gpre, gpost, wg, wu, wd, *, tm, layer, proj=None):
    t, d = x.shape
    row = pl.BlockSpec((tm, d), lambda i: (i, 0))
    args, specs = [x], [row]
    if proj is not None:
        o, wo, gmix = proj
        args += [o, wo, gmix]
        specs += [row, _full(wo.shape), _layer_row(gmix, layer)]
    args += [gpre, gpost, wg, wu, wd]
    specs += [_layer_row(a, layer) for a in (gpre, gpost, wg, wu, wd)]
    return pl.pallas_call(
        functools.partial(_ffn_kernel, ff_chunk=512, has_proj=proj is not None),
        out_shape=jax.ShapeDtypeStruct((t, d), F32),
        grid=(t // tm,),
        in_specs=specs,
        out_specs=row,
        compiler_params=_params("parallel"),
        name="ffn_proj" if proj is not None else "ffn",
    )(*args)


Q_SCALE = HEAD_DIM ** -0.5 * 1.4426950408889634


def _qkv_rows_kernel(x_ref, gpre_ref, w_ref, k_ref, v_ref, qkv_ref):
    d = x_ref.shape[1]
    h = _rms(x_ref[...], gpre_ref[...]).astype(BF16)
    qkv = _dot(h, w_ref[...])
    k_ref[...] = qkv[:, d:2 * d]
    v_ref[...] = qkv[:, 2 * d:]
    qkv_ref[:, :d] = (qkv[:, :d] * Q_SCALE).astype(BF16)
    qkv_ref[:, d:] = qkv[:, d:].astype(BF16)


def _qkv_rows(x, gpre, w_qkv, *, tm, layer):
    t, d = x.shape
    row = pl.BlockSpec((tm, d), lambda i: (i, 0))
    return pl.pallas_call(
        _qkv_rows_kernel,
        out_shape=[jax.ShapeDtypeStruct((t, d), F32), jax.ShapeDtypeStruct((t, d), F32),
                   jax.ShapeDtypeStruct((t, 3 * d), BF16)],
        grid=(t // tm,),
        in_specs=[row, _layer_row(gpre, layer), _full(w_qkv.shape)],
        out_specs=[row, row, pl.BlockSpec((tm, 3 * d), lambda i: (i, 0))],
        compiler_params=_params("parallel"),
        name="qkv_rows",
    )(x, gpre, w_qkv)


def _qkv_cols_kernel(x_ref, gpre_ref, wq_ref, wkt_ref, wvt_ref, q_ref, kt_ref, vt_ref, ktb_ref, vtb_ref):
    h = _rms(x_ref[...], gpre_ref[...]).astype(BF16)
    q_ref[...] = (_dot(h, wq_ref[...]) * Q_SCALE).astype(BF16)
    nt = (((1,), (1,)), ((), ()))
    kt = lax.dot_general(wkt_ref[...], h, nt, preferred_element_type=F32)
    kt_ref[...] = kt
    ktb_ref[...] = kt.astype(BF16)
    vt = lax.dot_general(wvt_ref[...], h, nt, preferred_element_type=F32)
    vt_ref[...] = vt
    vtb_ref[...] = vt.astype(BF16)


def _qkv_cols(x, gpre, wq, wkt, wvt, *, batch, tm, layer):
    t, d = x.shape
    seq = t // batch
    per = seq // tm
    row = pl.BlockSpec((tm, d), lambda i: (i, 0))
    col = pl.BlockSpec((None, d, tm), lambda i: (i // per, 0, i % per))
    return pl.pallas_call(
        _qkv_cols_kernel,
        out_shape=[jax.ShapeDtypeStruct((t, d), BF16),
                   jax.ShapeDtypeStruct((batch, d, seq), F32), jax.ShapeDtypeStruct((batch, d, seq), F32),
                   jax.ShapeDtypeStruct((batch, d, seq), BF16), jax.ShapeDtypeStruct((batch, d, seq), BF16)],
        grid=(t // tm,),
        in_specs=[row, _layer_row(gpre, layer), pl.BlockSpec((d, d), lambda i: (0, 0)), _full(wkt.shape),
                  _full(wvt.shape)],
        out_specs=[row, col, col, col, col],
        compiler_params=_params("parallel"),
        name="qkv_cols",
    )(x, gpre, wq, wkt, wvt)


ATTN_PAIRS = 2
LATER_DONE = 160.0
SOFTPLUS_LINEAR = 64.0


def _suffix_matrix(tk):
    j = lax.broadcasted_iota(jnp.int32, (tk, tk), 0)
    s = lax.broadcasted_iota(jnp.int32, (tk, tk), 1)
    return jnp.where(j > s, 1.0, 0.0).astype(BF16)


def _softplus2(z):
    return jnp.where(z > SOFTPLUS_LINEAR, z, jnp.log(1.0 + jnp.exp2(z)) * 1.4426950408889634)


def _sb_run(qs, ks, vs, later_ref, acc_ref, suffix, diagonal, transposed, stack):
    nt = (((1,), (1,)), ((), ()))
    n = len(qs)
    sub = suffix.shape[0]
    if transposed:
        zs = [_dot(q, k) for q, k in zip(qs, ks)]
    else:
        zs = [lax.dot_general(q, k, nt, preferred_element_type=F32) for q, k in zip(qs, ks)]
    tq, keys = zs[0].shape
    m = keys // sub
    sps = [_softplus2(z) for z in zs]
    if diagonal:
        col = lax.broadcasted_iota(jnp.int32, (tq, keys), 1)
        causal = col < lax.broadcasted_iota(jnp.int32, (tq, keys), 0) + (m - 1) * sub
        sps = [jnp.where(causal, sp, 0.0) for sp in sps]
    spb = [sp.astype(BF16) for sp in sps]
    if stack:
        order = [(j, c) for j in range(m) for c in range(n)]
        cs_all = _dot(jnp.concatenate([spb[c][:, j * sub:(j + 1) * sub] for j, c in order], axis=0), suffix)
        css = [[cs_all[(j * n + c) * tq:(j * n + c + 1) * tq] for c in range(n)] for j in range(m)]
    else:
        css = [[_dot(spb[c][:, j * sub:(j + 1) * sub], suffix) for c in range(n)] for j in range(m)]
    aa = []
    for c in range(n):
        later = later_ref[c]
        offs = [None] * m
        for j in reversed(range(m)):
            offs[j] = css[j][c] + jnp.concatenate([later] * (sub // LANES), axis=1)
            total = css[j][c][:, 0:1] + sps[c][:, j * sub:j * sub + 1]
            later = later + jnp.broadcast_to(total, later.shape)
        later_ref[c] = later
        a = jnp.exp2(zs[c] - sps[c] - jnp.concatenate(offs, axis=1))
        if diagonal:
            a = jnp.where(causal, a, 0.0)
        aa.append(a.astype(BF16))
    if transposed:
        pv = [lax.dot_general(a, v, nt, preferred_element_type=F32) for a, v in zip(aa, vs)]
    else:
        pv = [_dot(a, v) for a, v in zip(aa, vs)]
    for c in range(n):
        acc_ref[c] += pv[c]


def _sb_heads(q, later_ref, acc_ref, stack=False):
    tq = q.shape[0]
    pairs = q.shape[1] // LANES
    suffix = _suffix_matrix(ATTN_BLOCK)
    first = lax.broadcasted_iota(jnp.int32, (tq, LANES), 1) < HEAD_DIM
    zero = jnp.zeros((tq, LANES), q.dtype)
    qs = []
    for p in range(pairs):
        q2 = q[:, p * LANES:(p + 1) * LANES]
        qs += [jnp.where(first, q2, zero), jnp.where(first, zero, q2)]
    later_ref[...] = jnp.zeros_like(later_ref)
    acc_ref[...] = jnp.zeros_like(acc_ref)

    def run(kv_of_pair, diagonal=False, transposed=True):
        kvs = [kv for kv in map(kv_of_pair, range(pairs)) for _ in range(2)]
        _sb_run(qs, [kv[0] for kv in kvs], [kv[1] for kv in kvs], later_ref, acc_ref, suffix, diagonal,
                transposed, stack)

    def result():
        return jnp.concatenate([jnp.where(first, acc_ref[2 * p], acc_ref[2 * p + 1]) for p in range(pairs)], axis=1)

    return run, result


def _attn_prompt_kernel(q_ref, kt_ref, vt_ref, o_ref, later_ref, acc_ref):
    blk = ATTN_BLOCK

    def kv(kb, nblk):
        keys = pl.ds(pl.multiple_of(kb * blk, blk), nblk * blk)
        return lambda p: (kt_ref[p * LANES:(p + 1) * LANES, keys], vt_ref[p * LANES:(p + 1) * LANES, keys])

    def unfinished():
        return jnp.min(later_ref[...]) < LATER_DONE

    def query_block(qi, is_first):
        rows = pl.ds(pl.multiple_of(qi * blk, blk), blk)
        run, result = _sb_heads(q_ref[rows, :], later_ref, acc_ref)
        if is_first:
            run(kv(0, 1), diagonal=True)
        else:
            run(kv(qi - 1, 2), diagonal=True)

            def older(carry):
                kb, _ = carry
                run(kv(kb, 1))
                return kb - 1, unfinished()

            lax.while_loop(lambda c: (c[0] >= 0) & c[1], older, (qi - 2, unfinished()))
        o_ref[rows, :] = result().astype(BF16)

    query_block(0, True)
    pl.loop(1, q_ref.shape[0] // blk)(lambda qi: query_block(qi, False))


def _attn_prompt(q, kt, vt):
    t, d = q.shape
    batch, _, seq = kt.shape
    width = ATTN_PAIRS * LANES
    kvblk = pl.BlockSpec((None, width, seq), lambda b, p: (b, p, 0))
    qblk = pl.BlockSpec((seq, width), lambda b, p: (b, p))
    chains = pltpu.VMEM((2 * ATTN_PAIRS, ATTN_BLOCK, LANES), F32)
    return pl.pallas_call(
        _attn_prompt_kernel,
        out_shape=jax.ShapeDtypeStruct((t, d), BF16),
        grid=(batch, d // width),
        in_specs=[qblk, kvblk, kvblk],
        out_specs=qblk,
        scratch_shapes=[chains, chains],
        compiler_params=_params("parallel", "parallel"),
        name="attn_prompt",
    )(q, kt, vt)


def _attn_sample_kernel(q_ref, kn_ref, vn_ref, kct_ref, vct_ref, o_ref, later_ref, acc_ref):
    tq = q_ref.shape[0]
    run, result = _sb_heads(q_ref[...], later_ref, acc_ref, stack=True)
    pad = jnp.zeros((ATTN_BLOCK - tq, LANES), BF16)

    def own(p):
        cols = slice(p * LANES, (p + 1) * LANES)
        return (jnp.concatenate([kn_ref[:, cols], pad], axis=0), jnp.concatenate([vn_ref[:, cols], pad], axis=0))

    def cache(lo, hi):
        def kv(p):
            rows = slice(p * LANES, (p + 1) * LANES)
            return kct_ref[rows, lo:hi].astype(BF16), vct_ref[rows, lo:hi].astype(BF16)
        return kv

    past = kct_ref.shape[1]
    recent = past - 2 * ATTN_BLOCK
    run(own, diagonal=True, transposed=False)
    run(cache(recent, past))

    @pl.when(jnp.min(later_ref[...]) < LATER_DONE)
    def _():
        run(cache(0, recent))

    o_ref[...] = result().astype(BF16)


def _attn_sample(qkv, cache_kt, cache_vt, *, seq):
    t = qkv.shape[0]
    d = qkv.shape[1] // 3
    batch, _, past = cache_kt.shape
    width = ATTN_PAIRS * LANES
    ngrp = d // width
    cblk = pl.BlockSpec((None, width, past), lambda b, p: (b, p, 0))
    chains = pltpu.VMEM((2 * ATTN_PAIRS, seq, LANES), F32)
    return pl.pallas_call(
        _attn_sample_kernel,
        out_shape=jax.ShapeDtypeStruct((t, d), BF16),
        grid=(batch, ngrp),
        in_specs=[
            pl.BlockSpec((seq, width), lambda b, p: (b, p)),
            pl.BlockSpec((seq, width), lambda b, p: (b, ngrp + p)),
            pl.BlockSpec((seq, width), lambda b, p: (b, 2 * ngrp + p)),
            cblk, cblk,
        ],
        out_specs=pl.BlockSpec((seq, width), lambda b, p: (b, p)),
        scratch_shapes=[chains, chains],
        compiler_params=_params("parallel", "parallel"),
        name="attn_sample",
    )(qkv, qkv, qkv, cache_kt, cache_vt)


def _features_first(kv):
    b, s, h, e = kv.shape
    return jnp.transpose(kv, (0, 2, 3, 1)).reshape(b, h * e, s)


def _tokens_first(kvt, heads):
    b, d, s = kvt.shape
    return jnp.transpose(kvt.reshape(b, heads, d // heads, s), (0, 3, 1, 2))[None]


def _trunk(x, prev_pool, cache_kv, w, *, is_prompt, tm_mix, tm_tok):
    batch, seq, d = x.shape
    heads = d // HEAD_DIM
    t = batch * seq
    x = x.reshape(t, d)
    prev = jnp.pad(prev_pool, ((0, 0), (POOL_HALO - prev_pool.shape[1], 0), (0, 0)))
    chunk = 2 * CHUNK if is_prompt else seq
    outs = _mixer_ab(x, prev, w["norm_mix_pre"], w["norm_mix_post"], w["w_in_ab"], w["ln_v_g"],
                     w["ln_v_b"], w["w_spatial"], w["b_spatial"][:, :chunk].T, w["w_pool_map"],
                     w["pool_scale"], w["w_out_ab"], batch=batch, tm=tm_mix, chunk=chunk,
                     start_pos=0 if is_prompt else cache_kv[0].shape[1], emit_vn=not is_prompt, layer=0)
    x, tail = outs[0], outs[1]
    x = _ffn(x, w["norm_ffn_pre"], w["norm_ffn_post"], w["w_gate"], w["w_up"], w["w_down"], tm=tm_tok, layer=0)
    if is_prompt:
        q, kt, vt, ktb, vtb = _qkv_cols(x, w["norm_mix_pre"], w["w_qkv"], w["w_kt"], w["w_vt"],
                                        batch=batch, tm=tm_tok, layer=1)
        o = _attn_prompt(q, ktb, vtb)
        k, v = _tokens_first(kt, heads), _tokens_first(vt, heads)
    else:
        k, v, qkv = _qkv_rows(x, w["norm_mix_pre"], w["w_qkv"], tm=tm_tok, layer=1)
        o = _attn_sample(qkv, _features_first(cache_kv[0]), _features_first(cache_kv[1]), seq=seq)
        k, v = k.reshape(1, batch, seq, heads, HEAD_DIM), v.reshape(1, batch, seq, heads, HEAD_DIM)
    x = _ffn(x, w["norm_ffn_pre"], w["norm_ffn_post"], w["w_gate"], w["w_up"], w["w_down"], tm=tm_tok, layer=1,
             proj=(o, w["w_o_sb"], w["norm_mix_post"]))
    res = [x.reshape(batch, seq, d), tail[None, :, 1:, :], k, v]
    if not is_prompt:
        res.append(outs[2].reshape(1, batch, seq, -1))
    return res


def kernel(x_prompt, x_sample, state_pool, cache_k, cache_v, norm_mix_pre, norm_mix_post, norm_ffn_pre,
           norm_ffn_post, w_in_ab, ln_v_g, ln_v_b, w_spatial, b_spatial, w_pool_map, pool_scale, w_out_ab,
           w_qkv, w_o_sb, w_gate, w_up, w_down):
    depth, d = norm_mix_pre.shape
    assert depth == 2 and w_in_ab.shape[0] == 1 and w_qkv.shape[0] == 1
    wqkv = w_qkv[0].astype(BF16)
    w = dict(
        norm_mix_pre=norm_mix_pre.reshape(depth, 1, d), norm_mix_post=norm_mix_post.reshape(depth, 1, d),
        norm_ffn_pre=norm_ffn_pre.reshape(depth, 1, d), norm_ffn_post=norm_ffn_post.reshape(depth, 1, d),
        w_in_ab=w_in_ab[0].astype(BF16), ln_v_g=ln_v_g, ln_v_b=ln_v_b, w_spatial=w_spatial[0],
        b_spatial=b_spatial[0], w_pool_map=w_pool_map[0].astype(BF16), pool_scale=pool_scale,
        w_out_ab=w_out_ab[0].astype(BF16), w_qkv=wqkv, w_kt=wqkv[:, d:2 * d].T,
        w_vt=wqkv[:, 2 * d:].T, w_o_sb=w_o_sb[0].astype(BF16),
        w_gate=w_gate.astype(BF16), w_up=w_up.astype(BF16), w_down=w_down.astype(BF16),
    )
    batch = x_prompt.shape[0]
    zero_pool = jnp.zeros((batch,) + state_pool.shape[2:], F32)
    y_p, pool_p, k_p, v_p = _trunk(x_prompt, zero_pool, None, w, is_prompt=True, tm_mix=1024, tm_tok=512)
    y_s, pool_s, k_s, v_s, vn_s = _trunk(x_sample, state_pool[0], (cache_k[0], cache_v[0]), w,
                                         is_prompt=False, tm_mix=x_sample.shape[1],
                                         tm_tok=x_sample.shape[0] * x_sample.shape[1])
    return (y_p, y_s, pool_p, k_p, v_p, pool_s, k_s, v_s, vn_s)
```

```python
import functools

import jax
import jax.numpy as jnp
from jax import lax
from jax.experimental import pallas as pl
from jax.experimental.pallas import tpu as pltpu

F32 = jnp.float32
BF16 = jnp.bfloat16

EPS = 1e-6
CHUNK = 64
POOL_WINDOWS = (2, 4, 8, 16)
POOL_HALO = 16
LANES = 128
HEAD_DIM = 64
ATTN_BLOCK = 256
VMEM_LIMIT = 56 * 1024 * 1024


def _rms(x, g):
    return x * lax.rsqrt(jnp.mean(x * x, axis=-1, keepdims=True) + EPS) * g


def _dot(a, b):
    return jnp.dot(a, b, preferred_element_type=F32)


def _params(*sem):
    return pltpu.CompilerParams(dimension_semantics=sem, vmem_limit_bytes=VMEM_LIMIT)


def _full(shape):
    return pl.BlockSpec(shape, lambda *_: (0,) * len(shape))


def _layer_row(per_layer, layer):
    return pl.BlockSpec((None,) + per_layer.shape[1:], lambda *_: (layer, 0, 0))


def _mixer_ab_kernel(x_ref, prev_ref, gpre_ref, gpost_ref, win_ref, lng_ref, lnb_ref, ws_ref,
                     bs_ref, wmap_ref, pscale_ref, wout_ref, *rest,
                     tm, chunk, tiles_per_batch, start_pos, emit_vn):
    if emit_vn:
        x1_ref, tail_ref, vn_ref, ext_ref, cat_ref = rest
    else:
        x1_ref, tail_ref, ext_ref, cat_ref = rest
    aw = 4 * LANES
    tib = pl.program_id(0) % tiles_per_batch

    @pl.when(tib == 0)
    def _():
        ext_ref[0:POOL_HALO, :] = prev_ref[...]

    x = x_ref[...]
    h = _rms(x, gpre_ref[...]).astype(BF16)
    z = _dot(h, win_ref[...])
    zuv = z[:, :2 * aw]
    uv = 0.5 * zuv * (1.0 + lax.erf(zuv * (0.5 ** 0.5)))
    u, v = uv[:, :aw], uv[:, aw:]
    mu = jnp.mean(v, axis=-1, keepdims=True)
    vc = v - mu
    vn = vc * lax.rsqrt(jnp.mean(vc * vc, axis=-1, keepdims=True) + EPS) * lng_ref[...] + lnb_ref[...]
    if emit_vn:
        vn_ref[...] = vn

    ri = lax.broadcasted_iota(jnp.int32, (chunk, LANES), 0)
    ci = lax.broadcasted_iota(jnp.int32, (chunk, LANES), 1)
    allowed = (ci // CHUNK <= ri // CHUNK) & (ci < chunk)
    bias = bs_ref[...]
    vnb = vn.astype(BF16)
    for g in range(4):
        wg = jnp.where(allowed, ws_ref[g, 0:chunk, :], 0.0).astype(BF16)
        bg = bias[:, g:g + 1]
        for c in range(tm // chunk):
            vblk = vnb[c * chunk:(c + 1) * chunk, g * LANES:(g + 1) * LANES]
            if chunk < LANES:
                vblk = jnp.concatenate([vblk, jnp.zeros((LANES - chunk, LANES), BF16)], axis=0)
            s = _dot(wg, vblk) + bg
            ublk = u[c * chunk:(c + 1) * chunk, g * LANES:(g + 1) * LANES]
            cat_ref[c * chunk:(c + 1) * chunk, g * LANES:(g + 1) * LANES] = (ublk * s).astype(BF16)

    p = z[:, 2 * aw:]
    ext_ref[POOL_HALO:POOL_HALO + tm, :] = p
    e1 = ext_ref[...]
    e2 = e1 + pltpu.roll(e1, 1, axis=0)
    e4 = e2[:, LANES:] + pltpu.roll(e2[:, LANES:], 2, axis=0)
    e8 = e4[:, LANES:] + pltpu.roll(e4[:, LANES:], 4, axis=0)
    e16 = e8[:, LANES:] + pltpu.roll(e8[:, LANES:], 8, axis=0)
    wins = (e2[POOL_HALO:, :LANES], e4[POOL_HALO:, :LANES], e8[POOL_HALO:, :LANES], e16[POOL_HALO:, :])
    pos = start_pos + tib * tm + lax.broadcasted_iota(jnp.int32, (tm, 1), 0)
    pscale = pscale_ref[...]
    for g, w in enumerate(POOL_WINDOWS):
        inv_cnt = 1.0 / jnp.minimum(w, pos + 1).astype(F32)
        d = wins[g] * inv_cnt - p[:, g * LANES:(g + 1) * LANES]
        yb = _dot(d.astype(BF16), wmap_ref[g]) * pscale[:, g * LANES:(g + 1) * LANES]
        cat_ref[:, aw + g * LANES:aw + (g + 1) * LANES] = yb.astype(BF16)

    tail = ext_ref[tm:tm + POOL_HALO, :]
    ext_ref[0:POOL_HALO, :] = tail

    @pl.when(tib == tiles_per_batch - 1)
    def _():
        tail_ref[...] = tail

    y = _dot(cat_ref[...], wout_ref[...])
    x1_ref[...] = x + _rms(y, gpost_ref[...])


def _mixer_ab(x, prev, gpre, gpost, w_in, ln_g, ln_b, w_s, b_s, w_map, p_scale, w_out,
              *, batch, tm, chunk, start_pos, emit_vn, layer):
    t, d = x.shape
    aw = 4 * LANES
    tiles_per_batch = t // batch // tm
    kern = functools.partial(_mixer_ab_kernel, tm=tm, chunk=chunk, tiles_per_batch=tiles_per_batch,
                             start_pos=start_pos, emit_vn=emit_vn)
    out_shape = [jax.ShapeDtypeStruct((t, d), F32), jax.ShapeDtypeStruct((batch, POOL_HALO, aw), F32)]
    out_specs = [pl.BlockSpec((tm, d), lambda i: (i, 0)),
                 pl.BlockSpec((None, POOL_HALO, aw), lambda i: (i // tiles_per_batch, 0, 0))]
    if emit_vn:
        out_shape.append(jax.ShapeDtypeStruct((t, aw), F32))
        out_specs.append(pl.BlockSpec((tm, aw), lambda i: (i, 0)))
    return pl.pallas_call(
        kern,
        out_shape=out_shape,
        grid=(t // tm,),
        in_specs=[
            pl.BlockSpec((tm, d), lambda i: (i, 0)),
            pl.BlockSpec((None, POOL_HALO, aw), lambda i: (i // tiles_per_batch, 0, 0)),
            _layer_row(gpre, layer), _layer_row(gpost, layer), _full(w_in.shape), _full((1, aw)), _full((1, aw)),
            _full(w_s.shape), _full(b_s.shape), _full(w_map.shape), _full((1, aw)), _full(w_out.shape),
        ],
        out_specs=out_specs,
        scratch_shapes=[pltpu.VMEM((tm + POOL_HALO, aw), F32), pltpu.VMEM((tm, 2 * aw), BF16)],
        compiler_params=_params("arbitrary"),
        name="mixer_ab",
    )(x, prev, gpre, gpost, w_in, ln_g, ln_b, w_s, b_s, w_map, p_scale, w_out)


def _ffn_kernel(*refs, ff_chunk, sub_rows, has_proj):
    if has_proj:
        x_ref, o_ref, wo_ref, gmix_ref, gpre_ref, gpost_ref, wg_ref, wu_ref, wd_ref, out_ref = refs
    else:
        x_ref, gpre_ref, gpost_ref, wg_ref, wu_ref, wd_ref, out_ref = refs
    d_ff = wg_ref.shape[1]
    for r0 in range(0, x_ref.shape[0], sub_rows):
        rows = slice(r0, min(r0 + sub_rows, x_ref.shape[0]))
        x = x_ref[rows, :]
        if has_proj:
            x = x + _rms(_dot(o_ref[rows, :], wo_ref[...]), gmix_ref[...])
        h = _rms(x, gpre_ref[...]).astype(BF16)
        f = None
        for c0 in range(0, d_ff, ff_chunk):
            c1 = min(c0 + ff_chunk, d_ff)
            act = jax.nn.silu(_dot(h, wg_ref[:, c0:c1])) * _dot(h, wu_ref[:, c0:c1])
            part = _dot(act.astype(BF16), wd_ref[c0:c1, :])
            f = part if f is None else f + part
        out_ref[rows, :] = x + _rms(f, gpost_ref[...])


def _ffn(x, gpre, gpost, wg, wu, wd, *, tm, layer, proj=None):
    t, d = x.shape
    tile = min(t, 2 * tm)
    row = pl.BlockSpec((tile, d), lambda i: (i, 0))
    args, specs = [x], [row]
    if proj is not None:
        o, wo, gmix = proj
        args += [o, wo, gmix]
        specs += [row, _full(wo.shape), _layer_row(gmix, layer)]
    args += [gpre, gpost, wg, wu, wd]
    specs += [_layer_row(a, layer) for a in (gpre, gpost, wg, wu, wd)]
    return pl.pallas_call(
        functools.partial(_ffn_kernel, ff_chunk=512, sub_rows=tm, has_proj=proj is not None),
        out_shape=jax.ShapeDtypeStruct((t, d), F32),
        grid=(t // tile,),
        in_specs=specs,
        out_specs=row,
        compiler_params=_params("parallel"),
        name="ffn_proj" if proj is not None else "ffn",
    )(*args)


Q_SCALE = HEAD_DIM ** -0.5 * 1.4426950408889634


def _qkv_rows_kernel(x_ref, gpre_ref, w_ref, k_ref, v_ref, qkv_ref):
    d = x_ref.shape[1]
    h = _rms(x_ref[...], gpre_ref[...]).astype(BF16)
    qkv = _dot(h, w_ref[...])
    k_ref[...] = qkv[:, d:2 * d]
    v_ref[...] = qkv[:, 2 * d:]
    qkv_ref[:, :d] = (qkv[:, :d] * Q_SCALE).astype(BF16)
    qkv_ref[:, d:] = qkv[:, d:].astype(BF16)


def _qkv_rows(x, gpre, w_qkv, *, tm, layer):
    t, d = x.shape
    row = pl.BlockSpec((tm, d), lambda i: (i, 0))
    return pl.pallas_call(
        _qkv_rows_kernel,
        out_shape=[jax.ShapeDtypeStruct((t, d), F32), jax.ShapeDtypeStruct((t, d), F32),
                   jax.ShapeDtypeStruct((t, 3 * d), BF16)],
        grid=(t // tm,),
        in_specs=[row, _layer_row(gpre, layer), _full(w_qkv.shape)],
        out_specs=[row, row, pl.BlockSpec((tm, 3 * d), lambda i: (i, 0))],
        compiler_params=_params("parallel"),
        name="qkv_rows",
    )(x, gpre, w_qkv)


def _qkv_cols_kernel(x_ref, gpre_ref, wq_ref, wkt_ref, wvt_ref, q_ref, kt_ref, vt_ref, ktb_ref, vtb_ref, *, sub_rows):
    nt = (((1,), (1,)), ((), ()))
    for r0 in range(0, x_ref.shape[0], sub_rows):
        rows = slice(r0, r0 + sub_rows)
        h = _rms(x_ref[rows, :], gpre_ref[...]).astype(BF16)
        q_ref[rows, :] = (_dot(h, wq_ref[...]) * Q_SCALE).astype(BF16)
        kt = lax.dot_general(wkt_ref[...], h, nt, preferred_element_type=F32)
        kt_ref[:, rows] = kt
        ktb_ref[:, rows] = kt.astype(BF16)
        vt = lax.dot_general(wvt_ref[...], h, nt, preferred_element_type=F32)
        vt_ref[:, rows] = vt
        vtb_ref[:, rows] = vt.astype(BF16)


def _qkv_cols(x, gpre, wq, wkt, wvt, *, batch, tm, layer):
    t, d = x.shape
    seq = t // batch
    sub_rows, tm = tm, 2 * tm
    per = seq // tm
    row = pl.BlockSpec((tm, d), lambda i: (i, 0))
    col = pl.BlockSpec((None, d, tm), lambda i: (i // per, 0, i % per))
    return pl.pallas_call(
        functools.partial(_qkv_cols_kernel, sub_rows=sub_rows),
        out_shape=[jax.ShapeDtypeStruct((t, d), BF16),
                   jax.ShapeDtypeStruct((batch, d, seq), F32), jax.ShapeDtypeStruct((batch, d, seq), F32),
                   jax.ShapeDtypeStruct((batch, d, seq), BF16), jax.ShapeDtypeStruct((batch, d, seq), BF16)],
        grid=(t // tm,),
        in_specs=[row, _layer_row(gpre, layer), pl.BlockSpec((d, d), lambda i: (0, 0)), _full(wkt.shape),
                  _full(wvt.shape)],
        out_specs=[row, col, col, col, col],
        compiler_params=_params("parallel"),
        name="qkv_cols",
    )(x, gpre, wq, wkt, wvt)


ATTN_PAIRS = 2
LATER_DONE = 160.0
SOFTPLUS_LINEAR = 64.0


def _suffix_matrix(tk):
    j = lax.broadcasted_iota(jnp.int32, (tk, tk), 0)
    s = lax.broadcasted_iota(jnp.int32, (tk, tk), 1)
    return jnp.where(j > s, 1.0, 0.0).astype(BF16)


def _softplus2(z):
    return jnp.where(z > SOFTPLUS_LINEAR, z, jnp.log(1.0 + jnp.exp2(z)) * 1.4426950408889634)


def _sb_run(qs, ks, vs, later_ref, acc_ref, suffix, diagonal, transposed, stack):
    nt = (((1,), (1,)), ((), ()))
    n = len(qs)
    sub = suffix.shape[0]
    if transposed:
        zs = [_dot(q, k) for q, k in zip(qs, ks)]
    else:
        zs = [lax.dot_general(q, k, nt, preferred_element_type=F32) for q, k in zip(qs, ks)]
    tq, keys = zs[0].shape
    m = keys // sub
    sps = [_softplus2(z) for z in zs]
    if diagonal:
        col = lax.broadcasted_iota(jnp.int32, (tq, keys), 1)
        causal = col < lax.broadcasted_iota(jnp.int32, (tq, keys), 0) + (m - 1) * sub
        sps = [jnp.where(causal, sp, 0.0) for sp in sps]
    spb = [sp.astype(BF16) for sp in sps]
    if stack:
        order = [(j, c) for j in range(m) for c in range(n)]
        cs_all = _dot(jnp.concatenate([spb[c][:, j * sub:(j + 1) * sub] for j, c in order], axis=0), suffix)
        css = [[cs_all[(j * n + c) * tq:(j * n + c + 1) * tq] for c in range(n)] for j in range(m)]
    else:
        css = [[_dot(spb[c][:, j * sub:(j + 1) * sub], suffix) for c in range(n)] for j in range(m)]
    aa = []
    for c in range(n):
        later = later_ref[c]
        offs = [None] * m
        for j in reversed(range(m)):
            offs[j] = css[j][c] + jnp.concatenate([later] * (sub // LANES), axis=1)
            total = css[j][c][:, 0:1] + sps[c][:, j * sub:j * sub + 1]
            later = later + jnp.broadcast_to(total, later.shape)
        later_ref[c] = later
        a = jnp.exp2(zs[c] - sps[c] - jnp.concatenate(offs, axis=1))
        if diagonal:
            a = jnp.where(causal, a, 0.0)
        aa.append(a.astype(BF16))
    if transposed:
        pv = [lax.dot_general(a, v, nt, preferred_element_type=F32) for a, v in zip(aa, vs)]
    else:
        pv = [_dot(a, v) for a, v in zip(aa, vs)]
    for c in range(n):
        acc_ref[c] += pv[c]


def _sb_heads(q, later_ref, acc_ref, stack=False):
    tq = q.shape[0]
    pairs = q.shape[1] // LANES
    suffix = _suffix_matrix(ATTN_BLOCK)
    first = lax.broadcasted_iota(jnp.int32, (tq, LANES), 1) < HEAD_DIM
    zero = jnp.zeros((tq, LANES), q.dtype)
    qs = []
    for p in range(pairs):
        q2 = q[:, p * LANES:(p + 1) * LANES]
        qs += [jnp.where(first, q2, zero), jnp.where(first, zero, q2)]
    later_ref[...] = jnp.zeros_like(later_ref)
    acc_ref[...] = jnp.zeros_like(acc_ref)

    def run(kv_of_pair, diagonal=False, transposed=True):
        kvs = [kv for kv in map(kv_of_pair, range(pairs)) for _ in range(2)]
        _sb_run(qs, [kv[0] for kv in kvs], [kv[1] for kv in kvs], later_ref, acc_ref, suffix, diagonal,
                transposed, stack)

    def result():
        return jnp.concatenate([jnp.where(first, acc_ref[2 * p], acc_ref[2 * p + 1]) for p in range(pairs)], axis=1)

    return run, result


def _attn_prompt_kernel(q_ref, kt_ref, vt_ref, o_ref, later_ref, acc_ref):
    blk = ATTN_BLOCK

    def kv(kb, nblk):
        keys = pl.ds(pl.multiple_of(kb * blk, blk), nblk * blk)
        return lambda p: (kt_ref[p * LANES:(p + 1) * LANES, keys], vt_ref[p * LANES:(p + 1) * LANES, keys])

    def unfinished():
        return jnp.min(later_ref[...]) < LATER_DONE

    def query_block(qi, is_first):
        rows = pl.ds(pl.multiple_of(qi * blk, blk), blk)
        run, result = _sb_heads(q_ref[rows, :], later_ref, acc_ref)
        if is_first:
            run(kv(0, 1), diagonal=True)
        else:
            run(kv(qi - 1, 2), diagonal=True)

            def older(carry):
                kb, _ = carry
                run(kv(kb, 1))
                return kb - 1, unfinished()

            lax.while_loop(lambda c: (c[0] >= 0) & c[1], older, (qi - 2, unfinished()))
        o_ref[rows, :] = result().astype(BF16)

    query_block(0, True)
    pl.loop(1, q_ref.shape[0] // blk)(lambda qi: query_block(qi, False))


def _attn_prompt(q, kt, vt):
    t, d = q.shape
    batch, _, seq = kt.shape
    width = ATTN_PAIRS * LANES
    kvblk = pl.BlockSpec((None, width, seq), lambda b, p: (b, p, 0))
    qblk = pl.BlockSpec((seq, width), lambda b, p: (b, p))
    chains = pltpu.VMEM((2 * ATTN_PAIRS, ATTN_BLOCK, LANES), F32)
    return pl.pallas_call(
        _attn_prompt_kernel,
        out_shape=jax.ShapeDtypeStruct((t, d), BF16),
        grid=(batch, d // width),
        in_specs=[qblk, kvblk, kvblk],
        out_specs=qblk,
        scratch_shapes=[chains, chains],
        compiler_params=_params("parallel", "parallel"),
        name="attn_prompt",
    )(q, kt, vt)


def _attn_sample_kernel(q_ref, kn_ref, vn_ref, kct_ref, vct_ref, o_ref, later_ref, acc_ref):
    tq = q_ref.shape[0]
    run, result = _sb_heads(q_ref[...], later_ref, acc_ref, stack=True)
    pad = jnp.zeros((ATTN_BLOCK - tq, LANES), BF16)

    def own(p):
        cols = slice(p * LANES, (p + 1) * LANES)
        return (jnp.concatenate([kn_ref[:, cols], pad], axis=0), jnp.concatenate([vn_ref[:, cols], pad], axis=0))

    def cache(lo, hi):
        def kv(p):
            rows = slice(p * LANES, (p + 1) * LANES)
            return kct_ref[rows, lo:hi].astype(BF16), vct_ref[rows, lo:hi].astype(BF16)
        return kv

    past = kct_ref.shape[1]
    recent = past - 2 * ATTN_BLOCK
    run(own, diagonal=True, transposed=False)
    run(cache(recent, past))

    @pl.when(jnp.min(later_ref[...]) < LATER_DONE)
    def _():
        run(cache(0, recent))

    o_ref[...] = result().astype(BF16)


def _attn_sample(qkv, cache_kt, cache_vt, *, seq):
    t = qkv.shape[0]
    d = qkv.shape[1] // 3
    batch, _, past = cache_kt.shape
    width = ATTN_PAIRS * LANES
    ngrp = d // width
    cblk = pl.BlockSpec((None, width, past), lambda b, p: (b, p, 0))
    chains = pltpu.VMEM((2 * ATTN_PAIRS, seq, LANES), F32)
    return pl.pallas_call(
        _attn_sample_kernel,
        out_shape=jax.ShapeDtypeStruct((t, d), BF16),
        grid=(batch, ngrp),
        in_specs=[
            pl.BlockSpec((seq, width), lambda b, p: (b, p)),
            pl.BlockSpec((seq, width), lambda b, p: (b, ngrp + p)),
            pl.BlockSpec((seq, width), lambda b, p: (b, 2 * ngrp + p)),
            cblk, cblk,
        ],
        out_specs=pl.BlockSpec((seq, width), lambda b, p: (b, p)),
        scratch_shapes=[chains, chains],
        compiler_params=_params("parallel", "parallel"),
        name="attn_sample",
    )(qkv, qkv, qkv, cache_kt, cache_vt)


def _features_first(kv):
    b, s, h, e = kv.shape
    return jnp.transpose(kv, (0, 2, 3, 1)).reshape(b, h * e, s)


def _tokens_first(kvt, heads):
    b, d, s = kvt.shape
    return jnp.transpose(kvt.reshape(b, heads, d // heads, s), (0, 3, 1, 2))[None]


def _trunk(x, prev_pool, cache_kv, w, *, is_prompt, tm_mix, tm_tok):
    batch, seq, d = x.shape
    heads = d // HEAD_DIM
    t = batch * seq
    x = x.reshape(t, d)
    prev = jnp.pad(prev_pool, ((0, 0), (POOL_HALO - prev_pool.shape[1], 0), (0, 0)))
    chunk = 2 * CHUNK if is_prompt else seq
    outs = _mixer_ab(x, prev, w["norm_mix_pre"], w["norm_mix_post"], w["w_in_ab"], w["ln_v_g"],
                     w["ln_v_b"], w["w_spatial"], w["b_spatial"][:, :chunk].T, w["w_pool_map"],
                     w["pool_scale"], w["w_out_ab"], batch=batch, tm=tm_mix, chunk=chunk,
                     start_pos=0 if is_prompt else cache_kv[0].shape[1], emit_vn=not is_prompt, layer=0)
    x, tail = outs[0], outs[1]
    x = _ffn(x, w["norm_ffn_pre"], w["norm_ffn_post"], w["w_gate"], w["w_up"], w["w_down"], tm=tm_tok, layer=0)
    if is_prompt:
        q, kt, vt, ktb, vtb = _qkv_cols(x, w["norm_mix_pre"], w["w_qkv"], w["w_kt"], w["w_vt"],
                                        batch=batch, tm=tm_tok, layer=1)
        o = _attn_prompt(q, ktb, vtb)
        k, v = _tokens_first(kt, heads), _tokens_first(vt, heads)
    else:
        k, v, qkv = _qkv_rows(x, w["norm_mix_pre"], w["w_qkv"], tm=tm_tok, layer=1)
        o = _attn_sample(qkv, _features_first(cache_kv[0]), _features_first(cache_kv[1]), seq=seq)
        k, v = k.reshape(1, batch, seq, heads, HEAD_DIM), v.reshape(1, batch, seq, heads, HEAD_DIM)
    x = _ffn(x, w["norm_ffn_pre"], w["norm_ffn_post"], w["w_gate"], w["w_up"], w["w_down"], tm=tm_tok, layer=1,
             proj=(o, w["w_o_sb"], w["norm_mix_post"]))
    res = [x.reshape(batch, seq, d), tail[None, :, 1:, :], k, v]
    if not is_prompt:
        res.append(outs[2].reshape(1, batch, seq, -1))
    return res


def kernel(x_prompt, x_sample, state_pool, cache_k, cache_v, norm_mix_pre, norm_mix_post, norm_ffn_pre,
           norm_ffn_post, w_in_ab, ln_v_g, ln_v_b, w_spatial, b_spatial, w_pool_map, pool_scale, w_out_ab,
           w_qkv, w_o_sb, w_gate, w_up, w_down):
    depth, d = norm_mix_pre.shape
    assert depth == 2 and w_in_ab.shape[0] == 1 and w_qkv.shape[0] == 1
    wqkv = w_qkv[0].astype(BF16)
    w = dict(
        norm_mix_pre=norm_mix_pre.reshape(depth, 1, d), norm_mix_post=norm_mix_post.reshape(depth, 1, d),
        norm_ffn_pre=norm_ffn_pre.reshape(depth, 1, d), norm_ffn_post=norm_ffn_post.reshape(depth, 1, d),
        w_in_ab=w_in_ab[0].astype(BF16), ln_v_g=ln_v_g, ln_v_b=ln_v_b, w_spatial=w_spatial[0],
        b_spatial=b_spatial[0], w_pool_map=w_pool_map[0].astype(BF16), pool_scale=pool_scale,
        w_out_ab=w_out_ab[0].astype(BF16), w_qkv=wqkv, w_kt=wqkv[:, d:2 * d].T,
        w_vt=wqkv[:, 2 * d:].T, w_o_sb=w_o_sb[0].astype(BF16),
        w_gate=w_gate.astype(BF16), w_up=w_up.astype(BF16), w_down=w_down.astype(BF16),
    )
    batch = x_prompt.shape[0]
    zero_pool = jnp.zeros((batch,) + state_pool.shape[2:], F32)
    y_p, pool_p, k_p, v_p = _trunk(x_prompt, zero_pool, None, w, is_prompt=True, tm_mix=1024, tm_tok=512)
    y_s, pool_s, k_s, v_s, vn_s = _trunk(x_sample, state_pool[0], (cache_k[0], cache_v[0]), w,
                                         is_prompt=False, tm_mix=x_sample.shape[1],
                                         tm_tok=x_sample.shape[0] * x_sample.shape[1])
    return (y_p, y_s, pool_p, k_p, v_p, pool_s, k_s, v_s, vn_s)
```
